```python
import jax
import jax.numpy as jnp
from jax import lax
import numpy as np

D_MODEL = 2048
BATCH = 4
SEQ = 2048
DEPTH = 4
DEC_BATCH = 8
DEC_SEQ = 8
PAST_LEN = 16384
PAGE_SIZE = 128

N_EVEN = (DEPTH + 1) // 2
N_ODD = DEPTH // 2
D_A = D_MODEL // 2
HGRN_EXPAND = 128
H_A = D_A // HGRN_EXPAND
DK_A = HGRN_EXPAND
DV_A = D_A // H_A
HGRN_CHUNK = 64
D_B = D_MODEL - D_A
HD_B = 64
H_B = D_B // HD_B
R_DECAY = 64
R_AAA = 64
R_GATE = 160
B_PROJ = 3 * D_B + R_DECAY + R_AAA + R_GATE
IN_EVEN = 4 * D_A + B_PROJ
RWKV_GN_EPS = 64e-5
HD_C = 128
H_C = D_MODEL // HD_C
SB_QBLOCK = 128
SB_BIAS_HI = -4.0
SB_BIAS_LO = -10.0
D_FF = 5632
CONV_W = 3
ALPHA = (2 * DEPTH) ** 0.25
BETA = (8 * DEPTH) ** -0.25
LN_EPS = 1e-5
RMS_EPS = 1e-6

kernel_name = 'hybrid_hgrn2_rwkv7_stickbreak_step'


def layer_norm(x, g, b):
    xf = x.astype(jnp.float32)
    mu = jnp.mean(xf, axis=-1, keepdims=True)
    xc = xf - mu
    var = jnp.mean(xc * xc, axis=-1, keepdims=True)
    return (xc * lax.rsqrt(var + LN_EPS) * g + b).astype(x.dtype)


def hgrn_lower_bounds(lb_param):
    lbs = jnp.cumsum(jax.nn.softmax(lb_param.astype(jnp.float32), axis=0), axis=0)
    return lbs - lbs[0:1]


def gla_chunked(q, k, v, log_f, s0):
    bn, L, h, _ = q.shape
    dv = v.shape[-1]
    c = min(HGRN_CHUNK, L)
    pad = (-L) % c
    if pad:
        cfg = ((0, 0), (0, pad), (0, 0), (0, 0))
        q, k, v, log_f = (jnp.pad(t, cfg) for t in (q, k, v, log_f))
    n = (L + pad) // c

    def chunks(t):
        return t.reshape(bn, n, c, h, t.shape[-1]).transpose(1, 0, 3, 2, 4)

    incl = jnp.tril(jnp.ones((c, c), dtype=bool))[:, :, None]

    def step(s, inp):
        qc, kc, vc, gc = inp
        g = jnp.cumsum(gc, axis=2)
        o = jnp.einsum('bhtk,bhkv->bhtv', qc * jnp.exp(g), s)
        diff = jnp.where(incl, g[:, :, :, None, :] - g[:, :, None, :, :], -jnp.inf)
        att = jnp.sum(qc[:, :, :, None, :] * kc[:, :, None, :, :] * jnp.exp(diff), axis=-1)
        o = o + jnp.einsum('bhts,bhsv->bhtv', att, vc)
        g_last = g[:, :, -1:, :]
        s = (jnp.exp(g_last[:, :, 0, :])[..., None] * s
             + jnp.einsum('bhsk,bhsv->bhkv', kc * jnp.exp(g_last - g), vc))
        return s, o

    s, o = lax.scan(step, s0, (chunks(q), chunks(k), chunks(v), chunks(log_f)))
    o = o.transpose(1, 0, 3, 2, 4).reshape(bn, n * c, h, dv)[:, :L]
    return o, s


def rwkv7_scan(r, w, k, v, a, b, s0):
    def step(s, inp):
        rt, wt, kt, vt, at, bt = inp
        sa = jnp.einsum('bhvk,bhk->bhv', s, at)
        s = s * wt[:, :, None, :] + sa[..., None] * bt[:, :, None, :] + vt[..., None] * kt[:, :, None, :]
        return s, jnp.einsum('bhvk,bhk->bhv', s, rt)

    xs = tuple(jnp.moveaxis(t, 1, 0) for t in (r, w, k, v, a, b))
    s, ys = lax.scan(step, s0, xs)
    return jnp.moveaxis(ys, 0, 1), s


def hgrn_rwkv_mixer(x, s_hgrn, s_rwkv, s_shift, w_in, lb, hgrn_g, mu, w0, w2, a0, a2, g2,
                    k_k, k_a, r_k, gn_g, gn_b, w_out):
    f32 = jnp.float32
    bn, L, _ = x.shape
    p = (x @ w_in).astype(f32)
    qa, fa, ia, ga, pb = jnp.split(p, [D_A, 2 * D_A, 3 * D_A, 4 * D_A], axis=-1)
    lbf = lb.astype(f32)
    q = jax.nn.silu(qa)
    log_f = jnp.logaddexp(jnp.log(lbf), jnp.log1p(-lbf) + jax.nn.log_sigmoid(fa))
    k = (1.0 - lbf) * jax.nn.sigmoid(-fa)
    o_a, s_hgrn_new = gla_chunked(q.reshape(bn, L, H_A, DK_A), k.reshape(bn, L, H_A, DK_A),
                                  ia.reshape(bn, L, H_A, DV_A), log_f.reshape(bn, L, H_A, DK_A),
                                  s_hgrn.astype(f32))
    o_a = o_a * lax.rsqrt(jnp.mean(o_a * o_a, axis=-1, keepdims=True) + RMS_EPS) * hgrn_g.reshape(H_A, DV_A)
    o_a = o_a.reshape(bn, L, D_A) * jax.nn.silu(ga)
    prev = jnp.concatenate([s_shift.astype(f32)[:, None], pb[:, :-1]], axis=1)
    pm = pb + (prev - pb) * mu
    r, kb, v, wd, ad, gd = jnp.split(
        pm, [D_B, 2 * D_B, 3 * D_B, 3 * D_B + R_DECAY, 3 * D_B + R_DECAY + R_AAA], axis=-1)
    w = w0 + jnp.tanh(wd) @ w2
    decay = jnp.exp(-jnp.exp(-jax.nn.softplus(-w) - 0.5))
    a = jax.nn.sigmoid(a0 + ad @ a2)
    g = jax.nn.sigmoid(gd) @ g2

    def hb(t):
        return t.reshape(bn, L, H_B, HD_B)

    kk = hb(kb * k_k)
    kk = kk / jnp.maximum(jnp.sqrt(jnp.sum(kk * kk, axis=-1, keepdims=True)), 1e-12)
    kb = kb * (1.0 + (a - 1.0) * k_a)
    rh, kh, vh = hb(r), hb(kb), hb(v)
    y, s_rwkv_new = rwkv7_scan(rh, hb(decay), kh, vh, -kk, kk * hb(a), s_rwkv.astype(f32))
    ym = jnp.mean(y, axis=-1, keepdims=True)
    yc = y - ym
    y = yc * lax.rsqrt(jnp.mean(yc * yc, axis=-1, keepdims=True) + RWKV_GN_EPS)
    y = y.reshape(bn, L, D_B) * gn_g + gn_b
    bonus = jnp.sum(rh * kh * r_k, axis=-1, keepdims=True) * vh
    o_b = (y + bonus.reshape(bn, L, D_B)) * g
    out = jnp.concatenate([o_a, o_b], axis=-1).astype(x.dtype) @ w_out
    return out, s_hgrn_new.astype(x.dtype), s_rwkv_new.astype(x.dtype), pb[:, -1].astype(x.dtype)


def stick_breaking(q, k_parts, v_parts, q_off, bias):
    f32 = jnp.float32
    lq = q.shape[1]
    scale = HD_C ** -0.5
    bias4 = bias.astype(f32)[None, :, None, None]
    outs = []
    for qs in range(0, lq, SB_QBLOCK):
        qb = q[:, qs:qs + SB_QBLOCK]
        nq = qb.shape[1]
        kend = max(q_off + qs + nq - 1, 1)
        ks_, vs_, off = [], [], 0
        for kp, vp in zip(k_parts, v_parts):
            take = min(kp.shape[1], kend - off)
            if take > 0:
                ks_.append(kp[:, :take])
                vs_.append(vp[:, :take])
            off += kp.shape[1]
        z = jnp.concatenate([jnp.einsum('bqhd,bkhd->bhqk', qb, kt) for kt in ks_], axis=-1).astype(f32) * scale + bias4
        tpos = q_off + qs + jnp.arange(nq)
        spos = jnp.arange(z.shape[-1])
        causal = spos[None, :] < tpos[:, None]
        sp = jnp.where(causal, jax.nn.softplus(z), 0.0)
        log_a = jnp.where(causal, z - lax.cumsum(sp, axis=3, reverse=True), -jnp.inf)
        att = jnp.exp(log_a)
        o = None
        off = 0
        for vt in vs_:
            n = vt.shape[1]
            part = jnp.einsum('bhqk,bkhd->bqhd', att[..., off:off + n], vt.astype(f32))
            o = part if o is None else o + part
            off += n
        outs.append(o)
    return jnp.concatenate(outs, axis=1)


def sb_mixer(x, k_past_parts, v_past_parts, q_off, w_qkv, w_out, bias):
    bn, L, _ = x.shape
    q, k, v = jnp.split(x @ w_qkv, 3, axis=-1)
    q = q.reshape(bn, L, H_C, HD_C)
    k = k.reshape(bn, L, H_C, HD_C)
    v = v.reshape(bn, L, H_C, HD_C)
    o = stick_breaking(q, k_past_parts + [k], v_past_parts + [v], q_off, bias)
    return o.reshape(bn, L, D_MODEL).astype(x.dtype) @ w_out, k, v


def conv_ffn(x, buf, w_up, conv_w, conv_b, w_down):
    L = x.shape[1]
    h = x @ w_up
    hp = jnp.concatenate([buf.astype(h.dtype), h], axis=1)
    hc = conv_b + conv_w[0] * hp[:, 0:L]
    for j in range(1, CONV_W):
        hc = hc + conv_w[j] * hp[:, j:j + L]
    u, g = jnp.split(hc, 2, axis=-1)
    y = (jax.nn.gelu(g, approximate=False) * u) @ w_down
    return y, hp[:, L:]


def setup_inputs(seed: int = 0) -> dict:
    key = jax.random.key(seed)
    ks = iter(jax.random.split(key, 40))
    f32 = jnp.float32

    def nrm(shape, s):
        return jax.random.normal(next(ks), shape, f32) * s

    def unif(shape):
        return jax.random.uniform(next(ks), shape, f32)

    n_pages = PAST_LEN // PAGE_SIZE
    n_used = DEC_BATCH * n_pages
    n_pool = (5 * n_used + 3) // 4
    inp = {}
    inp['x_prompt'] = nrm((BATCH, SEQ, D_MODEL), 1.0)
    inp['x_sample'] = nrm((DEC_BATCH, DEC_SEQ, D_MODEL), 1.0)
    inp['cache_k'] = nrm((N_ODD, n_pool, PAGE_SIZE, H_C, HD_C), 1.0)
    inp['cache_v'] = nrm((N_ODD, n_pool, PAGE_SIZE, H_C, HD_C), 1.0)
    perm = jax.random.permutation(next(ks), n_pool)
    inp['page_table'] = perm[:n_used].reshape(DEC_BATCH, n_pages).astype(jnp.int32)
    inp['state_hgrn'] = nrm((N_EVEN, DEC_BATCH, H_A, DK_A, DV_A), 1.0)
    inp['state_rwkv'] = nrm((N_EVEN, DEC_BATCH, H_B, HD_B, HD_B), 0.3)
    inp['state_rwkv_shift'] = nrm((N_EVEN, DEC_BATCH, B_PROJ), 1.0)
    inp['state_ffn_conv'] = nrm((DEPTH, DEC_BATCH, CONV_W - 1, 2 * D_FF), 1.0)
    inp['w_in_even'] = nrm((N_EVEN, D_MODEL, IN_EVEN), D_MODEL ** -0.5)
    inp['hgrn_lb_param'] = nrm((N_EVEN, D_A), 1.0)
    inp['hgrn_norm_g'] = 1.0 + nrm((N_EVEN, D_A), 0.02)
    inp['rwkv_mu'] = unif((N_EVEN, B_PROJ))
    inp['rwkv_w0'] = -6.5 + 5.0 * unif((N_EVEN, D_B))
    inp['rwkv_w2'] = nrm((N_EVEN, R_DECAY, D_B), 0.5 * R_DECAY ** -0.5)
    inp['rwkv_a0'] = nrm((N_EVEN, D_B), 0.1)
    inp['rwkv_a2'] = nrm((N_EVEN, R_AAA, D_B), 0.5 * R_AAA ** -0.5)
    inp['rwkv_g2'] = nrm((N_EVEN, R_GATE, D_B), R_GATE ** -0.5)
    inp['rwkv_k_k'] = 0.85 + nrm((N_EVEN, D_B), 0.02)
    inp['rwkv_k_a'] = 1.0 + nrm((N_EVEN, D_B), 0.02)
    inp['rwkv_r_k'] = nrm((N_EVEN, H_B, HD_B), 0.1)
    inp['rwkv_gn_g'] = 1.0 + nrm((N_EVEN, D_B), 0.02)
    inp['rwkv_gn_b'] = nrm((N_EVEN, D_B), 0.02)
    inp['w_out_even'] = nrm((N_EVEN, D_A + D_B, D_MODEL), BETA * (D_A + D_B) ** -0.5)
    inp['w_qkv_odd'] = nrm((N_ODD, D_MODEL, 3 * D_MODEL), D_MODEL ** -0.5)
    inp['w_out_odd'] = nrm((N_ODD, D_MODEL, D_MODEL), BETA * D_MODEL ** -0.5)
    inp['sb_bias'] = jnp.linspace(SB_BIAS_HI, SB_BIAS_LO, H_C, dtype=f32)[None] + nrm((N_ODD, H_C), 0.1)
    inp['ln1_g'] = 1.0 + nrm((DEPTH, D_MODEL), 0.02)
    inp['ln1_b'] = nrm((DEPTH, D_MODEL), 0.02)
    inp['ln2_g'] = 1.0 + nrm((DEPTH, D_MODEL), 0.02)
    inp['ln2_b'] = nrm((DEPTH, D_MODEL), 0.02)
    inp['ffn_up'] = nrm((DEPTH, D_MODEL, 2 * D_FF), D_MODEL ** -0.5)
    inp['ffn_conv_w'] = nrm((DEPTH, CONV_W, 2 * D_FF), CONV_W ** -0.5)
    inp['ffn_conv_b'] = nrm((DEPTH, 2 * D_FF), 0.02)
    inp['ffn_down'] = nrm((DEPTH, D_FF, D_MODEL), BETA * D_FF ** -0.5)
    return inp


def reference(x_prompt, x_sample, cache_k, cache_v, page_table, state_hgrn, state_rwkv,
              state_rwkv_shift, state_ffn_conv, w_in_even, hgrn_lb_param, hgrn_norm_g, rwkv_mu,
              rwkv_w0, rwkv_w2, rwkv_a0, rwkv_a2, rwkv_g2, rwkv_k_k, rwkv_k_a, rwkv_r_k, rwkv_gn_g,
              rwkv_gn_b, w_out_even, w_qkv_odd, w_out_odd, sb_bias, ln1_g, ln1_b, ln2_g, ln2_b,
              ffn_up, ffn_conv_w, ffn_conv_b, ffn_down):
    lbs = hgrn_lower_bounds(hgrn_lb_param)
    xp, xs = x_prompt, x_sample
    bp, bs = xp.shape[0], xs.shape[0]
    kp_l, vp_l, ks_l, vs_l = [], [], [], []
    hp_l, hs_l, rp_l, rs_l = [], [], [], []
    shp_l, shs_l, cp_l, cs_l = [], [], [], []
    for l in range(DEPTH):
        if l % 2 == 0:
            e = l // 2
            wts = (w_in_even[e], lbs[e], hgrn_norm_g[e], rwkv_mu[e], rwkv_w0[e], rwkv_w2[e],
                   rwkv_a0[e], rwkv_a2[e], rwkv_g2[e], rwkv_k_k[e], rwkv_k_a[e], rwkv_r_k[e],
                   rwkv_gn_g[e], rwkv_gn_b[e], w_out_even[e])
            mp, hp, rp, shp = hgrn_rwkv_mixer(
                xp, jnp.zeros((bp, H_A, DK_A, DV_A), xp.dtype), jnp.zeros((bp, H_B, HD_B, HD_B), xp.dtype),
                jnp.zeros((bp, B_PROJ), xp.dtype), *wts)
            ms, hs, rs, shs = hgrn_rwkv_mixer(xs, state_hgrn[e], state_rwkv[e], state_rwkv_shift[e], *wts)
            hp_l.append(hp)
            hs_l.append(hs)
            rp_l.append(rp)
            rs_l.append(rs)
            shp_l.append(shp)
            shs_l.append(shs)
        else:
            o = l // 2
            mp, kp, vp = sb_mixer(xp, [], [], 0, w_qkv_odd[o], w_out_odd[o], sb_bias[o])
            k_past = cache_k[o][page_table].reshape(bs, -1, H_C, HD_C)
            v_past = cache_v[o][page_table].reshape(bs, -1, H_C, HD_C)
            ms, k_new, v_new = sb_mixer(xs, [k_past], [v_past], k_past.shape[1],
                                        w_qkv_odd[o], w_out_odd[o], sb_bias[o])
            kp_l.append(kp)
            vp_l.append(vp)
            ks_l.append(k_new)
            vs_l.append(v_new)
        xp = layer_norm(ALPHA * xp + mp, ln1_g[l], ln1_b[l])
        xs = layer_norm(ALPHA * xs + ms, ln1_g[l], ln1_b[l])
        fp, cp = conv_ffn(xp, jnp.zeros((bp, CONV_W - 1, 2 * D_FF), xp.dtype),
                          ffn_up[l], ffn_conv_w[l], ffn_conv_b[l], ffn_down[l])
        fs, cs = conv_ffn(xs, state_ffn_conv[l], ffn_up[l], ffn_conv_w[l], ffn_conv_b[l], ffn_down[l])
        cp_l.append(cp)
        cs_l.append(cs)
        xp = layer_norm(ALPHA * xp + fp, ln2_g[l], ln2_b[l])
        xs = layer_norm(ALPHA * xs + fs, ln2_g[l], ln2_b[l])
    return (xp, xs, jnp.stack(kp_l), jnp.stack(vp_l), jnp.stack(ks_l), jnp.stack(vs_l),
            jnp.stack(hp_l), jnp.stack(hs_l), jnp.stack(rp_l), jnp.stack(rs_l),
            jnp.stack(shp_l), jnp.stack(shs_l), jnp.stack(cp_l), jnp.stack(cs_l))
```

```python
import functools

import jax
import jax.numpy as jnp
import numpy as np
from jax import lax
from jax.experimental import pallas as pl
from jax.experimental.pallas import tpu as pltpu

F32 = jnp.float32
BF16 = jnp.bfloat16

D_MODEL = 2048
DEPTH = 4
PAGE_SIZE = 128
D_A = D_MODEL // 2
DK_A = 128
H_A = D_A // DK_A
DV_A = D_A // H_A
D_B = D_MODEL - D_A
HD_B = 64
H_B = D_B // HD_B
R_DECAY = 64
R_AAA = 64
R_GATE = 160
B_PROJ = 3 * D_B + R_DECAY + R_AAA + R_GATE
RWKV_GN_EPS = 64e-5
HD_C = 128
H_C = D_MODEL // HD_C
D_FF = 5632
CONV_W = 3
ALPHA = (2 * DEPTH) ** 0.25
LN_EPS = 1e-5
RMS_EPS = 1e-6

LANES = 128
HGRN_CHUNK = 16
VMEM_LIMIT = 56 * 1024 * 1024


def _cparams(n_axes):
    return pltpu.CompilerParams(dimension_semantics=("arbitrary",) * n_axes,
                                vmem_limit_bytes=VMEM_LIMIT)


def _split2(x):
    hi = x.astype(BF16)
    lo = (x - hi.astype(F32)).astype(BF16)
    return hi, lo


def _split3(x):
    hi = x.astype(BF16)
    r = x - hi.astype(F32)
    mid = r.astype(BF16)
    lo = (r - mid.astype(F32)).astype(BF16)
    return hi, mid, lo


def _softplus(x):
    return jnp.maximum(x, 0.0) + jnp.log1p(jnp.exp(-jnp.abs(x)))


def _sigmoid(x):
    return 1.0 / (1.0 + jnp.exp(-x))


def _mm_body(x_ref, w_ref, o_ref, wb_ref):
    @pl.when(pl.program_id(1) == 0)
    def _():
        wb_ref[...] = w_ref[...].astype(BF16)

    o_ref[...] = jnp.dot(x_ref[...].astype(BF16), wb_ref[...],
                         preferred_element_type=F32).astype(o_ref.dtype)


def matmul(x, w, layer, *, n_out=None, col_off=0, tm=1024, tn=512, out_dtype=F32):
    m, k = x.shape
    n = n_out or w.shape[2]
    tm = min(tm, m)
    assert m % tm == 0 and col_off % tn == 0
    joff = col_off // tn
    return pl.pallas_call(
        _mm_body,
        grid=(pl.cdiv(n, tn), m // tm),
        in_specs=[pl.BlockSpec((tm, k), lambda j, i: (i, 0)),
                  pl.BlockSpec((None, k, tn), lambda j, i: (layer, 0, j + joff))],
        out_specs=pl.BlockSpec((tm, tn), lambda j, i: (i, j)),
        out_shape=jax.ShapeDtypeStruct((m, n), out_dtype),
        scratch_shapes=[pltpu.VMEM((k, tn), BF16)],
        compiler_params=_cparams(2),
        name="matmul",
    )(x, w)


def _ln_body(x_ref, m_ref, g_ref, b_ref, o_ref, ob_ref):
    v = ALPHA * x_ref[...] + m_ref[...]
    mu = jnp.mean(v, axis=-1, keepdims=True)
    vc = v - mu
    var = jnp.mean(vc * vc, axis=-1, keepdims=True)
    y = vc * lax.rsqrt(var + LN_EPS) * g_ref[...] + b_ref[...]
    o_ref[...] = y
    ob_ref[...] = y.astype(BF16)


def add_layer_norm(x, mix, g, b, layer, *, tm=512):
    m, d = x.shape
    tm = min(tm, m)
    g3 = g.reshape(g.shape[0], 1, d)
    b3 = b.reshape(b.shape[0], 1, d)
    row = pl.BlockSpec((tm, d), lambda i: (i, 0))
    par = pl.BlockSpec((None, 1, d), lambda i: (layer, 0, 0))
    return pl.pallas_call(
        _ln_body,
        grid=(m // tm,),
        in_specs=[row, row, par, par],
        out_specs=[row, row],
        out_shape=[jax.ShapeDtypeStruct((m, d), F32), jax.ShapeDtypeStruct((m, d), BF16)],
        compiler_params=_cparams(1),
        name="add_layer_norm",
    )(x, mix, g3, b3)


def _conv_geglu_body(hu_ref, hg_ref, bu_ref, bg_ref, cwu_ref, cwg_ref, cbu_ref, cbg_ref,
                     a_ref, su_ref, sg_ref):
    rows = hu_ref.shape[0]
    ridx = lax.broadcasted_iota(jnp.int32, hu_ref.shape, 0)

    def conv(h_ref, buf_ref, cw_ref, cb_ref, s_ref):
        h = h_ref[...]
        buf = buf_ref[...]
        hm1 = jnp.where(ridx < 1, buf[1:2, :], pltpu.roll(h, 1, 0))
        hm2 = jnp.where(ridx < 1, buf[0:1, :], jnp.where(ridx < 2, buf[1:2, :], pltpu.roll(h, 2, 0)))
        cw = cw_ref[...]
        s_ref[...] = h[rows - 2:rows, :]
        return cb_ref[...] + cw[0:1, :] * hm2 + cw[1:2, :] * hm1 + cw[2:3, :] * h

    u = conv(hu_ref, bu_ref, cwu_ref, cbu_ref, su_ref)
    g = conv(hg_ref, bg_ref, cwg_ref, cbg_ref, sg_ref)
    gelu = 0.5 * g * (1.0 + lax.erf(g * (2.0 ** -0.5)))
    a_ref[...] = (gelu * u).astype(a_ref.dtype)


def conv_geglu(h, buf, conv_w, conv_b, layer, seq, *, tn=512):
    m = h.shape[0]
    bsz = m // seq
    nj = D_FF // tn
    if buf is None:
        buf = jnp.zeros((1, bsz, CONV_W - 1, 2 * D_FF), F32)
        blayer = 0
    else:
        blayer = layer
    cb3 = conv_b.reshape(conv_b.shape[0], 1, 2 * D_FF)
    hu = pl.BlockSpec((seq, tn), lambda b, j: (b, j))
    hg = pl.BlockSpec((seq, tn), lambda b, j: (b, j + nj))
    bu = pl.BlockSpec((None, None, CONV_W - 1, tn), lambda b, j: (blayer, b, 0, j))
    bg = pl.BlockSpec((None, None, CONV_W - 1, tn), lambda b, j: (blayer, b, 0, j + nj))
    cwu = pl.BlockSpec((None, CONV_W, tn), lambda b, j: (layer, 0, j))
    cwg = pl.BlockSpec((None, CONV_W, tn), lambda b, j: (layer, 0, j + nj))
    cbu = pl.BlockSpec((None, 1, tn), lambda b, j: (layer, 0, j))
    cbg = pl.BlockSpec((None, 1, tn), lambda b, j: (layer, 0, j + nj))
    st = pl.BlockSpec((None, CONV_W - 1, tn), lambda b, j: (b, 0, j))
    a, su, sg = pl.pallas_call(
        _conv_geglu_body,
        grid=(bsz, nj),
        in_specs=[hu, hg, bu, bg, cwu, cwg, cbu, cbg],
        out_specs=[pl.BlockSpec((seq, tn), lambda b, j: (b, j)), st, st],
        out_shape=[jax.ShapeDtypeStruct((m, D_FF), BF16),
                   jax.ShapeDtypeStruct((bsz, CONV_W - 1, D_FF), F32),
                   jax.ShapeDtypeStruct((bsz, CONV_W - 1, D_FF), F32)],
        compiler_params=_cparams(2),
        name="conv_geglu",
    )(h, h, buf, buf, conv_w, conv_w, cb3, cb3)
    return a, jnp.concatenate([su, sg], axis=-1)


_NT = (((1,), (1,)), ((), ()))
_TN = (((0,), (0,)), ((), ()))


def _hgrn_body(q_ref, f_ref, i_ref, g_ref, lb_ref, ng_ref, s0_ref, tri_ref, blk_ref,
               o_ref, s_ref, *, valid):
    seq = q_ref.shape[0]
    slab = tri_ref.shape[0]
    c = HGRN_CHUNK
    lb = lb_ref[...]
    log_lb = jnp.log(lb)
    log_1mlb = jnp.log1p(-lb)
    ng = ng_ref[...]
    tri = tri_ref[...]
    blk = blk_ref[...]
    causal = tri > 0

    def slab_step(si, st):
        r0 = pl.multiple_of(si * slab, slab)
        rows = pl.ds(r0, slab)
        qa = q_ref[rows, :]
        fa = f_ref[rows, :]
        ia = i_ref[rows, :]
        ga = g_ref[rows, :]
        q = qa * _sigmoid(qa)
        x2 = log_1mlb + (jnp.minimum(fa, 0.0) - jnp.log1p(jnp.exp(-jnp.abs(fa))))
        mx = jnp.maximum(log_lb, x2)
        log_f = mx + jnp.log(jnp.exp(log_lb - mx) + jnp.exp(x2 - mx))
        k = (1.0 - lb) * _sigmoid(-fa)
        if valid < seq:
            live = (r0 + lax.broadcasted_iota(jnp.int32, fa.shape, 0)) < valid
            log_f = jnp.where(live, log_f, 0.0)
            k = jnp.where(live, k, 0.0)
        parts = _split3(log_f)
        g = sum(jnp.dot(tri, p, preferred_element_type=F32) for p in parts)
        gl = sum(jnp.dot(blk, p, preferred_element_type=F32) for p in parts)
        qg = (q * jnp.exp(g)).astype(BF16)
        kinv = (k * jnp.exp(-g)).astype(BF16)
        kg = (k * jnp.exp(gl - g)).astype(BF16)
        vb = ia.astype(BF16)
        att = lax.dot_general(qg, kinv, _NT, preferred_element_type=F32)
        att = jnp.where(causal, att, 0.0).astype(BF16)
        o_in = jnp.dot(att, vb, preferred_element_type=F32)
        dec = jnp.exp(gl)
        outs = []
        for ci in range(slab // c):
            rs = slice(ci * c, (ci + 1) * c)
            o_c = o_in[rs] + lax.dot_general(qg[rs], st.astype(BF16), _NT,
                                             preferred_element_type=F32)
            outs.append(o_c)
            u = lax.dot_general(vb[rs], kg[rs], _TN, preferred_element_type=F32)
            st = st * dec[ci * c:ci * c + 1, :] + u
        o = jnp.concatenate(outs, axis=0) if len(outs) > 1 else outs[0]
        o = o * lax.rsqrt(jnp.mean(o * o, axis=-1, keepdims=True) + RMS_EPS) * ng
        o_ref[rows, :] = (o * (ga * _sigmoid(ga))).astype(o_ref.dtype)
        return st

    st = lax.fori_loop(0, seq // slab, slab_step, s0_ref[...].T)
    s_ref[...] = st.T


def hgrn_mixer(pa, lb, norm_g, s0, layer, seq, valid):
    m = pa.shape[0]
    bsz = m // seq
    slab = min(LANES, seq)
    assert seq % slab == 0 and slab % HGRN_CHUNK == 0
    if s0 is None:
        s0 = jnp.zeros((1, bsz, H_A, DK_A, DV_A), F32)
        slayer = 0
    else:
        slayer = layer
    r = np.arange(slab)
    same = (r[:, None] // HGRN_CHUNK) == (r[None, :] // HGRN_CHUNK)
    tri = jnp.asarray(same & (r[None, :] <= r[:, None]), BF16)
    blk = jnp.asarray(same, BF16)
    col = lambda off: pl.BlockSpec((seq, DK_A), lambda b, h: (b, h + off))
    vec = pl.BlockSpec((1, DK_A), lambda b, h: (0, h))
    const = pl.BlockSpec((slab, slab), lambda b, h: (0, 0))
    o, s = pl.pallas_call(
        functools.partial(_hgrn_body, valid=valid),
        grid=(bsz, H_A),
        in_specs=[col(0), col(H_A), col(2 * H_A), col(3 * H_A), vec, vec,
                  pl.BlockSpec((None, None, None, DK_A, DV_A), lambda b, h: (slayer, b, h, 0, 0)),
                  const, const],
        out_specs=[pl.BlockSpec((seq, DV_A), lambda b, h: (b, h)),
                   pl.BlockSpec((None, None, DK_A, DV_A), lambda b, h: (b, h, 0, 0))],
        out_shape=[jax.ShapeDtypeStruct((m, D_A), BF16),
                   jax.ShapeDtypeStruct((bsz, H_A, DK_A, DV_A), F32)],
        compiler_params=_cparams(2),
        name="hgrn_mixer",
    )(pa, pa, pa, pa, lb.reshape(1, D_A), norm_g.reshape(1, D_A), s0, tri, blk)
    return o, s


LORA_IN = 3 * LANES
_LORA_USED = R_DECAY + R_AAA + R_GATE


def _seg_sum(x, sel):
    hi, lo = _split2(x)
    return (jnp.dot(hi, sel, preferred_element_type=F32)
            + jnp.dot(lo, sel, preferred_element_type=F32))


def _rwkv_prep_body(pb_ref, prev_ref, tail_ref, tprev_ref, mu_ref, mut_ref, w0_ref, a0_ref,
                    lora_ref, kk_ref, ka_ref, rk_ref, e_ref, sel_ref,
                    w_o, k_o, v_o, a_o, b_o, wr_o, br_o, kr_o, bonus_o, gate_o):
    pb = pb_ref[:, 0:3 * D_B]
    pm = pb + (prev_ref[:, 0:3 * D_B] - pb) * mu_ref[...]
    r = pm[:, 0:D_B]
    kb = pm[:, D_B:2 * D_B]
    v = pm[:, 2 * D_B:3 * D_B]
    tl = tail_ref[...]
    tm_ = tl + (tprev_ref[...] - tl) * mut_ref[...]
    col = lax.broadcasted_iota(jnp.int32, tm_.shape, 1)
    act = jnp.where(col < R_DECAY, jnp.tanh(tm_),
                    jnp.where(col < R_DECAY + R_AAA, tm_, _sigmoid(tm_)))
    act = jnp.where(col < _LORA_USED, act, 0.0).astype(BF16)
    up = jnp.dot(act, lora_ref[...], preferred_element_type=F32)
    w = w0_ref[...] + up[:, 0:D_B]
    decay = jnp.exp(-jnp.exp(-_softplus(-w) - 0.5))
    a = _sigmoid(a0_ref[...] + up[:, D_B:2 * D_B])
    gate_o[...] = up[:, 2 * D_B:3 * D_B]
    e = e_ref[...]
    sel = sel_ref[...]
    kk = kb * kk_ref[...]
    kk = kk / jnp.maximum(jnp.sqrt(_seg_sum(kk * kk, e)), 1e-12)
    k2 = kb * (1.0 + (a - 1.0) * ka_ref[...])
    bvec = kk * a
    w_o[...] = decay
    k_o[...] = k2
    v_o[...] = v
    a_o[...] = -kk
    b_o[...] = bvec
    wr_o[...] = decay * r
    br_o[...] = _seg_sum(bvec * r, sel)
    kr_o[...] = _seg_sum(k2 * r, sel)
    bonus_o[...] = _seg_sum(r * k2 * rk_ref[...], e) * v


def rwkv_prep(pb, prev, p, layer, *, tm=256):
    m = pb.shape[0]
    tm = min(tm, m)
    pad = ((0, 0), (0, LORA_IN - _LORA_USED))
    tail = jnp.pad(pb[:, 3 * D_B:], pad)
    tprev = jnp.pad(prev[:, 3 * D_B:], pad)
    mu = p["rwkv_mu"][layer]
    mu_main = mu[:3 * D_B].reshape(1, 3 * D_B)
    mu_tail = jnp.pad(mu[3 * D_B:].reshape(1, -1), pad)
    lora = jnp.zeros((LORA_IN, 3 * D_B), F32)
    lora = lora.at[0:R_DECAY, 0:D_B].set(p["rwkv_w2"][layer])
    lora = lora.at[R_DECAY:R_DECAY + R_AAA, D_B:2 * D_B].set(p["rwkv_a2"][layer])
    lora = lora.at[R_DECAY + R_AAA:_LORA_USED, 2 * D_B:].set(p["rwkv_g2"][layer]).astype(BF16)
    c = np.arange(D_B)
    e = jnp.asarray((c[:, None] // HD_B) == (c[None, :] // HD_B), BF16)
    sel = jnp.asarray((c[:, None] // HD_B) == np.arange(LANES)[None, :], BF16)
    vec = lambda name: p[name][layer].reshape(1, D_B)
    row = lambda n: pl.BlockSpec((tm, n), lambda i: (i, 0))
    full = lambda a, b: pl.BlockSpec((a, b), lambda i: (0, 0))
    big = jax.ShapeDtypeStruct((m, D_B), F32)
    small = jax.ShapeDtypeStruct((m, LANES), F32)
    return pl.pallas_call(
        _rwkv_prep_body,
        grid=(m // tm,),
        in_specs=[row(B_PROJ), row(B_PROJ), row(LORA_IN), row(LORA_IN), full(1, 3 * D_B),
                  full(1, LORA_IN), full(1, D_B), full(1, D_B), full(LORA_IN, 3 * D_B),
                  full(1, D_B), full(1, D_B), full(1, D_B), full(D_B, D_B), full(D_B, LANES)],
        out_specs=[row(D_B)] * 6 + [row(LANES)] * 2 + [row(D_B)] * 2,
        out_shape=[big] * 6 + [small] * 2 + [big] * 2,
        compiler_params=_cparams(1),
        name="rwkv_prep",
    )(pb, prev, tail, tprev, mu_main, mu_tail, vec("rwkv_w0"), vec("rwkv_a0"), lora,
      vec("rwkv_k_k"), vec("rwkv_k_a"), p["rwkv_r_k"][layer].reshape(1, D_B), e, sel)


def _rwkv_scan_body(w_ref, k_ref, a_ref, b_ref, wr_ref, v_ref, sc_ref, s0_ref, y_ref, sout_ref,
                    s_scr):
    tb = w_ref.shape[0]
    nk = s_scr.shape[0]
    nacc = 4

    @pl.when(pl.program_id(0) == 0)
    def _():
        s_scr[...] = s0_ref[...]

    def step(t, carry):
        sa = [None] * nacc
        yp = [None] * nacc
        for kx in range(nk):
            sk = s_scr[kx]
            pa = sk * a_ref[t, kx:kx + 1, :]
            py = sk * wr_ref[t, kx:kx + 1, :]
            j = kx % nacc
            sa[j] = pa if sa[j] is None else sa[j] + pa
            yp[j] = py if yp[j] is None else yp[j] + py
        sa_t = (sa[0] + sa[1]) + (sa[2] + sa[3])
        yp_t = (yp[0] + yp[1]) + (yp[2] + yp[3])
        vt = v_ref[t]
        for kx in range(nk):
            s_scr[kx] = (s_scr[kx] * w_ref[t, kx:kx + 1, :] + sa_t * b_ref[t, kx:kx + 1, :]
                         + vt * k_ref[t, kx:kx + 1, :])
        y_ref[t] = yp_t + sa_t * sc_ref[t, 0:1, :] + vt * sc_ref[t, 1:2, :]
        return carry

    lax.fori_loop(0, tb, step, 0)

    @pl.when(pl.program_id(0) == pl.num_programs(0) - 1)
    def _():
        sout_ref[...] = s_scr[...]


def rwkv_scan(w, k, a, b, wr, v, sc, s0, *, tb=16):
    seq = w.shape[0]
    vs = v.shape[1]
    tb = min(tb, seq)
    assert seq % tb == 0
    kblk = pl.BlockSpec((tb, HD_B, LANES), lambda i: (i, 0, 0))
    vblk = pl.BlockSpec((tb, vs, LANES), lambda i: (i, 0, 0))
    sblk = pl.BlockSpec((HD_B, vs, LANES), lambda i: (0, 0, 0))
    return pl.pallas_call(
        _rwkv_scan_body,
        grid=(seq // tb,),
        in_specs=[kblk] * 5 + [vblk, pl.BlockSpec((tb, 2, LANES), lambda i: (i, 0, 0)), sblk],
        out_specs=[vblk, sblk],
        out_shape=[jax.ShapeDtypeStruct((seq, vs, LANES), F32),
                   jax.ShapeDtypeStruct((HD_B, vs, LANES), F32)],
        scratch_shapes=[pltpu.VMEM((HD_B, vs, LANES), F32)],
        compiler_params=_cparams(1),
        name="rwkv_scan",
    )(w, k, a, b, wr, v, sc, s0)


def _rwkv_post_body(y_ref, bonus_ref, gate_ref, g_ref, b_ref, e_ref, o_ref):
    e = e_ref[...]
    y = y_ref[...]
    yc = y - _seg_sum(y, e) * (1.0 / HD_B)
    var = _seg_sum(yc * yc, e) * (1.0 / HD_B)
    yn = yc * lax.rsqrt(var + RWKV_GN_EPS) * g_ref[...] + b_ref[...]
    o_ref[...] = ((yn + bonus_ref[...]) * gate_ref[...]).astype(o_ref.dtype)


def rwkv_post(y, bonus, gate, gn_g, gn_b, *, tm=512):
    m = y.shape[0]
    tm = min(tm, m)
    c = np.arange(D_B)
    e = jnp.asarray((c[:, None] // HD_B) == (c[None, :] // HD_B), BF16)
    row = pl.BlockSpec((tm, D_B), lambda i: (i, 0))
    vec = pl.BlockSpec((1, D_B), lambda i: (0, 0))
    return pl.pallas_call(
        _rwkv_post_body,
        grid=(m // tm,),
        in_specs=[row, row, row, vec, vec, pl.BlockSpec((D_B, D_B), lambda i: (0, 0))],
        out_specs=row,
        out_shape=jax.ShapeDtypeStruct((m, D_B), BF16),
        compiler_params=_cparams(1),
        name="rwkv_post",
    )(y, bonus, gate, gn_g.reshape(1, D_B), gn_b.reshape(1, D_B), e)


def rwkv_mixer(pb, shift, s0, p, layer, seq):
    m = pb.shape[0]
    bsz = m // seq
    nch = bsz * H_B
    nsplit = LANES // nch
    vs = HD_B // nsplit
    assert nch * nsplit == LANES
    pb3 = pb.reshape(bsz, seq, B_PROJ)
    first = jnp.zeros((bsz, 1, B_PROJ), F32) if shift is None else shift[:, None]
    prev = jnp.concatenate([first, pb3[:, :-1]], axis=1).reshape(m, B_PROJ)
    w, k, v, a, b, wr, br, kr, bonus, gate = rwkv_prep(pb, prev, p, layer)

    def chain_major(x):
        x = x.reshape(bsz, seq, H_B, HD_B).transpose(1, 3, 0, 2).reshape(seq, HD_B, nch)
        return jnp.tile(x, (1, 1, nsplit))

    def scalars(x):
        x = x[:, :H_B].reshape(bsz, seq, H_B).transpose(1, 0, 2).reshape(seq, nch)
        return jnp.tile(x, (1, nsplit))

    v_c = v.reshape(bsz, seq, H_B, nsplit, vs).transpose(1, 4, 3, 0, 2).reshape(seq, vs, LANES)
    sc = jnp.stack([scalars(br), scalars(kr)], axis=1)
    if s0 is None:
        s0_c = jnp.zeros((HD_B, vs, LANES), F32)
    else:
        s0_c = s0.reshape(bsz, H_B, nsplit, vs, HD_B).transpose(4, 3, 2, 0, 1).reshape(HD_B, vs, LANES)
    y_c, s_c = rwkv_scan(chain_major(w), chain_major(k), chain_major(a), chain_major(b),
                         chain_major(wr), v_c, sc, s0_c)
    y = y_c.reshape(seq, vs, nsplit, bsz, H_B).transpose(3, 0, 4, 2, 1).reshape(m, D_B)
    s_new = s_c.reshape(HD_B, vs, nsplit, bsz, H_B).transpose(3, 4, 2, 1, 0).reshape(bsz, H_B, HD_B, HD_B)
    o = rwkv_post(y, bonus, gate, p["rwkv_gn_g"][layer], p["rwkv_gn_b"][layer])
    return o, s_new


def even_mixer(xb, seq, s_hgrn, s_rwkv, s_shift, lbs, p, e):
    m = xb.shape[0]
    bsz = m // seq
    pa = matmul(xb, p["w_in_even"], e, n_out=4 * D_A)
    pb = matmul(xb, p["w_in_even"], e, n_out=B_PROJ, col_off=4 * D_A)
    seq_a = -(-seq // HGRN_CHUNK) * HGRN_CHUNK
    pa_p = pa
    if seq_a != seq:
        pa_p = jnp.pad(pa.reshape(bsz, seq, -1), ((0, 0), (0, seq_a - seq), (0, 0))).reshape(bsz * seq_a, -1)
    o_a, h_new = hgrn_mixer(pa_p, lbs[e], p["hgrn_norm_g"][e], s_hgrn, e, seq_a, seq)
    if seq_a != seq:
        o_a = o_a.reshape(bsz, seq_a, D_A)[:, :seq].reshape(m, D_A)
    o_b, r_new = rwkv_mixer(pb, None if s_shift is None else s_shift[e],
                            None if s_rwkv is None else s_rwkv[e], p, e, seq)
    mix = matmul(jnp.concatenate([o_a, o_b], axis=-1), p["w_out_even"], e)
    return mix, h_new, r_new, pb.reshape(bsz, seq, B_PROJ)[:, -1]


SB_TQ = 256
SB_TK = 128
SB_SCALE = HD_C ** -0.5


def _sb_prompt_body(bias_ref, q_ref, k_ref, v_ref, ltri_ref, o_ref, *, layer):
    tq, tk = SB_TQ, SB_TK
    qi = pl.program_id(2)
    bias = bias_ref[layer, pl.program_id(1)]
    q = q_ref[...].astype(BF16)
    ltri = ltri_ref[...]
    tpos = qi * tq + lax.broadcasted_iota(jnp.int32, (tq, tk), 0)
    lane = lax.broadcasted_iota(jnp.int32, (tq, tk), 1)

    def kv_step(jj, carry):
        o, c = carry
        j = (qi + 1) * (tq // tk) - 1 - jj
        k0 = pl.multiple_of(j * tk, tk)
        kj = k_ref[pl.ds(k0, tk), :].astype(BF16)
        vj = v_ref[pl.ds(k0, tk), :].astype(BF16)
        z = lax.dot_general(q, kj, _NT, preferred_element_type=F32) * SB_SCALE + bias
        causal = (k0 + lane) < tpos
        sp = jnp.where(causal, _softplus(z), 0.0)
        hi, lo = _split2(sp)
        cs = (jnp.dot(hi, ltri, preferred_element_type=F32)
              + jnp.dot(lo, ltri, preferred_element_type=F32))
        att = jnp.where(causal, jnp.exp(z - cs - c), 0.0)
        o = o + jnp.dot(att.astype(BF16), vj, preferred_element_type=F32)
        c = c + jnp.sum(sp, axis=1, keepdims=True)
        return o, c

    o, _ = lax.fori_loop(0, (qi + 1) * (tq // tk), kv_step,
                         (jnp.zeros((tq, HD_C), F32), jnp.zeros((tq, 1), F32)))
    o_ref[...] = o.astype(o_ref.dtype)


def sb_prompt(qkv, bias, layer, seq):
    m = qkv.shape[0]
    bsz = m // seq
    tq = SB_TQ
    assert seq % tq == 0
    r = np.arange(SB_TK)
    ltri = jnp.asarray(r[:, None] >= r[None, :], BF16)
    nq = seq // tq
    return pl.pallas_call(
        functools.partial(_sb_prompt_body, layer=layer),
        grid=(bsz, H_C, nq),
        in_specs=[pl.BlockSpec(memory_space=pltpu.SMEM),
                  pl.BlockSpec((tq, HD_C), lambda b, h, i: (b * nq + i, h)),
                  pl.BlockSpec((seq, HD_C), lambda b, h, i: (b, H_C + h)),
                  pl.BlockSpec((seq, HD_C), lambda b, h, i: (b, 2 * H_C + h)),
                  pl.BlockSpec((SB_TK, SB_TK), lambda b, h, i: (0, 0))],
        out_specs=pl.BlockSpec((tq, HD_C), lambda b, h, i: (b * nq + i, h)),
        out_shape=jax.ShapeDtypeStruct((m, D_MODEL), BF16),
        compiler_params=_cparams(3),
        name="sb_prompt",
    )(bias, qkv, qkv, qkv, ltri)


def _sb_sample_body(pt_ref, kn_ref, vn_ref, kp_ref, vp_ref, qbd_ref, bias_ref, utri_ref, o_ref,
                    acc_ref, c_ref):
    p = pl.program_id(1)
    nq = LANES // H_C
    utri = utri_ref[...]

    def block(k_ref, v_ref, new_keys):
        kb = k_ref[...].astype(BF16)
        z = jnp.dot(kb, qbd_ref[...], preferred_element_type=F32) * SB_SCALE + bias_ref[...]
        sp = _softplus(z)
        if new_keys:
            s_idx = lax.broadcasted_iota(jnp.int32, z.shape, 0)
            q_idx = lax.broadcasted_iota(jnp.int32, z.shape, 1) % nq
            causal = s_idx < q_idx
            sp = jnp.where(causal, sp, 0.0)
        hi, lo = _split2(sp)
        cs = (jnp.dot(utri, hi, preferred_element_type=F32)
              + jnp.dot(utri, lo, preferred_element_type=F32))
        att = jnp.exp(z - cs - c_ref[...])
        if new_keys:
            att = jnp.where(causal, att, 0.0)
        acc_ref[...] += lax.dot_general(att.astype(BF16), v_ref[...].astype(BF16), _TN,
                                        preferred_element_type=F32)
        c_ref[...] += cs[0:1, :]

    @pl.when(p == 0)
    def _():
        acc_ref[...] = jnp.zeros_like(acc_ref)
        c_ref[...] = jnp.zeros_like(c_ref)
        block(kn_ref, vn_ref, True)

    @pl.when(p > 0)
    def _():
        block(kp_ref, vp_ref, False)

    @pl.when(p == pl.num_programs(1) - 1)
    def _():
        for h in range(H_C):
            o_ref[:, h * HD_C:(h + 1) * HD_C] = acc_ref[h * nq:(h + 1) * nq,
                                                        h * HD_C:(h + 1) * HD_C].astype(o_ref.dtype)


def sb_sample(qkv, cache_k, cache_v, page_table, bias, layer, seq):
    m = qkv.shape[0]
    bsz = m // seq
    n_pages = page_table.shape[1]
    assert seq * H_C == LANES
    n_pool = cache_k.shape[1]
    ck = cache_k.reshape(cache_k.shape[0], n_pool, PAGE_SIZE, D_MODEL)
    cv = cache_v.reshape(cache_v.shape[0], n_pool, PAGE_SIZE, D_MODEL)
    q3 = qkv[:, :D_MODEL].reshape(bsz, seq, H_C, HD_C)
    qt = q3.transpose(0, 2, 3, 1).reshape(bsz, D_MODEL, 1, seq)
    hsel = (np.arange(D_MODEL)[:, None] // HD_C) == np.arange(H_C)[None, :]
    qbd = jnp.where(jnp.asarray(hsel)[None, :, :, None], qt, 0.0).reshape(bsz, D_MODEL, LANES).astype(BF16)
    pad = ((0, 0), (0, PAGE_SIZE - seq), (0, 0))
    kn = jnp.pad(qkv[:, D_MODEL:2 * D_MODEL].reshape(bsz, seq, D_MODEL), pad)
    vn = jnp.pad(qkv[:, 2 * D_MODEL:].reshape(bsz, seq, D_MODEL), pad)
    bias_row = jnp.repeat(bias[layer], seq).reshape(1, LANES)
    r = np.arange(PAGE_SIZE)
    utri = jnp.asarray(r[None, :] >= r[:, None], BF16)

    def page(b, p, pt):
        return (layer, pt[b, n_pages - jnp.maximum(p, 1)], 0, 0)

    grid_spec = pltpu.PrefetchScalarGridSpec(
        num_scalar_prefetch=1,
        grid=(bsz, n_pages + 1),
        in_specs=[pl.BlockSpec((None, PAGE_SIZE, D_MODEL), lambda b, p, pt: (b, 0, 0)),
                  pl.BlockSpec((None, PAGE_SIZE, D_MODEL), lambda b, p, pt: (b, 0, 0)),
                  pl.BlockSpec((None, None, PAGE_SIZE, D_MODEL), page),
                  pl.BlockSpec((None, None, PAGE_SIZE, D_MODEL), page),
                  pl.BlockSpec((None, D_MODEL, LANES), lambda b, p, pt: (b, 0, 0)),
                  pl.BlockSpec((1, LANES), lambda b, p, pt: (0, 0)),
                  pl.BlockSpec((PAGE_SIZE, PAGE_SIZE), lambda b, p, pt: (0, 0))],
        out_specs=pl.BlockSpec((seq, D_MODEL), lambda b, p, pt: (b, 0)),
        scratch_shapes=[pltpu.VMEM((LANES, D_MODEL), F32), pltpu.VMEM((1, LANES), F32)],
    )
    return pl.pallas_call(
        _sb_sample_body,
        grid_spec=grid_spec,
        out_shape=jax.ShapeDtypeStruct((m, D_MODEL), BF16),
        compiler_params=_cparams(2),
        name="sb_sample",
    )(page_table, kn, vn, ck, cv, qbd, bias_row, utri)


def sb_mixer(xb, seq, cache_k, cache_v, page_table, p, o):
    m = xb.shape[0]
    bsz = m // seq
    qkv = matmul(xb, p["w_qkv_odd"], o)
    if cache_k is None:
        att = sb_prompt(qkv, p["sb_bias"], o, seq)
    else:
        att = sb_sample(qkv, cache_k, cache_v, page_table, p["sb_bias"], o, seq)
    mix = matmul(att, p["w_out_odd"], o)
    k_new = qkv[:, D_MODEL:2 * D_MODEL].reshape(bsz, seq, H_C, HD_C)
    v_new = qkv[:, 2 * D_MODEL:].reshape(bsz, seq, H_C, HD_C)
    return mix, k_new, v_new


def conv_ffn(xb, seq, buf, p, layer):
    h = matmul(xb, p["ffn_up"], layer)
    a, buf_new = conv_geglu(h, buf, p["ffn_conv_w"], p["ffn_conv_b"], layer, seq)
    return matmul(a, p["ffn_down"], layer, tm=512), buf_new


def _hgrn_lower_bounds(lb_param):
    lbs = jnp.cumsum(jax.nn.softmax(lb_param.astype(F32), axis=0), axis=0)
    return lbs - lbs[0:1]


def kernel(x_prompt, x_sample, cache_k, cache_v, page_table, state_hgrn, state_rwkv, state_rwkv_shift, state_ffn_conv, w_in_even, hgrn_lb_param, hgrn_norm_g, rwkv_mu, rwkv_w0, rwkv_w2, rwkv_a0, rwkv_a2, rwkv_g2, rwkv_k_k, rwkv_k_a, rwkv_r_k, rwkv_gn_g, rwkv_gn_b, w_out_even, w_qkv_odd, w_out_odd, sb_bias, ln1_g, ln1_b, ln2_g, ln2_b, ffn_up, ffn_conv_w, ffn_conv_b, ffn_down):
    p = dict(w_in_even=w_in_even, hgrn_norm_g=hgrn_norm_g, rwkv_mu=rwkv_mu, rwkv_w0=rwkv_w0,
             rwkv_w2=rwkv_w2, rwkv_a0=rwkv_a0, rwkv_a2=rwkv_a2, rwkv_g2=rwkv_g2, rwkv_k_k=rwkv_k_k,
             rwkv_k_a=rwkv_k_a, rwkv_r_k=rwkv_r_k, rwkv_gn_g=rwkv_gn_g, rwkv_gn_b=rwkv_gn_b,
             w_out_even=w_out_even, w_qkv_odd=w_qkv_odd, w_out_odd=w_out_odd, sb_bias=sb_bias,
             ffn_up=ffn_up, ffn_conv_w=ffn_conv_w, ffn_conv_b=ffn_conv_b, ffn_down=ffn_down)
    lbs = _hgrn_lower_bounds(hgrn_lb_param)
    bp, lp, _ = x_prompt.shape
    bs, ls, _ = x_sample.shape
    groups = [
        dict(x=x_prompt.reshape(bp * lp, D_MODEL), seq=lp, sample=False),
        dict(x=x_sample.reshape(bs * ls, D_MODEL), seq=ls, sample=True),
    ]
    for g in groups:
        g["xb"] = g["x"].astype(BF16)
        g["k"], g["v"], g["hgrn"], g["rwkv"], g["shift"], g["conv"] = [], [], [], [], [], []
    for layer in range(DEPTH):
        for g in groups:
            seq, sample = g["seq"], g["sample"]
            if layer % 2 == 0:
                e = layer // 2
                mix, h_new, r_new, sh_new = even_mixer(
                    g["xb"], seq, state_hgrn if sample else None, state_rwkv if sample else None,
                    state_rwkv_shift if sample else None, lbs, p, e)
                g["hgrn"].append(h_new)
                g["rwkv"].append(r_new)
                g["shift"].append(sh_new)
            else:
                o = layer // 2
                mix, k_new, v_new = sb_mixer(g["xb"], seq, cache_k if sample else None,
                                             cache_v if sample else None, page_table, p, o)
                g["k"].append(k_new)
                g["v"].append(v_new)
            g["x"], g["xb"] = add_layer_norm(g["x"], mix, ln1_g, ln1_b, layer)
            f, c_new = conv_ffn(g["xb"], seq, state_ffn_conv if sample else None, p, layer)
            g["conv"].append(c_new)
            g["x"], g["xb"] = add_layer_norm(g["x"], f, ln2_g, ln2_b, layer)
    gp, gs = groups
    st = jnp.stack
    return (gp["x"].reshape(bp, lp, D_MODEL), gs["x"].reshape(bs, ls, D_MODEL),
            st(gp["k"]), st(gp["v"]), st(gs["k"]), st(gs["v"]),
            st(gp["hgrn"]), st(gs["hgrn"]), st(gp["rwkv"]), st(gs["rwkv"]),
            st(gp["shift"]), st(gs["shift"]), st(gp["conv"]), st(gs["conv"]))
```

```python
import functools

import jax
import jax.numpy as jnp
import numpy as np
from jax import lax
from jax.experimental import pallas as pl
from jax.experimental.pallas import tpu as pltpu

F32 = jnp.float32
BF16 = jnp.bfloat16

D_MODEL = 2048
DEPTH = 4
PAGE_SIZE = 128
D_A = D_MODEL // 2
DK_A = 128
H_A = D_A // DK_A
DV_A = D_A // H_A
D_B = D_MODEL - D_A
HD_B = 64
H_B = D_B // HD_B
R_DECAY = 64
R_AAA = 64
R_GATE = 160
B_PROJ = 3 * D_B + R_DECAY + R_AAA + R_GATE
RWKV_GN_EPS = 64e-5
HD_C = 128
H_C = D_MODEL // HD_C
D_FF = 5632
CONV_W = 3
ALPHA = (2 * DEPTH) ** 0.25
LN_EPS = 1e-5
RMS_EPS = 1e-6

LANES = 128
SUBLANES = 8
HGRN_CHUNK = 16
HEAD_GROUP = 4
VMEM_LIMIT = 56 * 1024 * 1024


def _cparams(n_axes):
    return pltpu.CompilerParams(dimension_semantics=("arbitrary",) * n_axes,
                                vmem_limit_bytes=VMEM_LIMIT)


def _split2(x):
    hi = x.astype(BF16)
    lo = (x - hi.astype(F32)).astype(BF16)
    return hi, lo


def _split3(x):
    hi = x.astype(BF16)
    r = x - hi.astype(F32)
    mid = r.astype(BF16)
    lo = (r - mid.astype(F32)).astype(BF16)
    return hi, mid, lo


def _softplus(x):
    return jnp.maximum(x, 0.0) + jnp.log1p(jnp.exp(-jnp.abs(x)))


def _sigmoid(x):
    return 1.0 / (1.0 + jnp.exp(-x))


def _mm_body(x_ref, w_ref, o_ref, wb_ref):
    @pl.when(pl.program_id(1) == 0)
    def _():
        wb_ref[...] = w_ref[...].astype(BF16)

    o_ref[...] = jnp.dot(x_ref[...].astype(BF16), wb_ref[...],
                         preferred_element_type=F32).astype(o_ref.dtype)


def matmul(x, w, layer, *, n_out=None, col_off=0, tm=1024, tn=512, out_dtype=F32):
    m, k = x.shape
    n = n_out or w.shape[2]
    tm = min(tm, m)
    assert m % tm == 0 and col_off % tn == 0
    joff = col_off // tn
    return pl.pallas_call(
        _mm_body,
        grid=(pl.cdiv(n, tn), m // tm),
        in_specs=[pl.BlockSpec((tm, k), lambda j, i: (i, 0)),
                  pl.BlockSpec((None, k, tn), lambda j, i: (layer, 0, j + joff))],
        out_specs=pl.BlockSpec((tm, tn), lambda j, i: (i, j)),
        out_shape=jax.ShapeDtypeStruct((m, n), out_dtype),
        scratch_shapes=[pltpu.VMEM((k, tn), BF16)],
        compiler_params=_cparams(2),
        name="matmul",
    )(x, w)


def _ln_body(x_ref, m_ref, g_ref, b_ref, o_ref, ob_ref):
    v = ALPHA * x_ref[...] + m_ref[...]
    mu = jnp.mean(v, axis=-1, keepdims=True)
    vc = v - mu
    var = jnp.mean(vc * vc, axis=-1, keepdims=True)
    y = vc * lax.rsqrt(var + LN_EPS) * g_ref[...] + b_ref[...]
    o_ref[...] = y
    ob_ref[...] = y.astype(BF16)


def add_layer_norm(x, mix, g, b, layer, *, tm=512):
    m, d = x.shape
    tm = min(tm, m)
    g3 = g.reshape(g.shape[0], 1, d)
    b3 = b.reshape(b.shape[0], 1, d)
    row = pl.BlockSpec((tm, d), lambda i: (i, 0))
    par = pl.BlockSpec((None, 1, d), lambda i: (layer, 0, 0))
    return pl.pallas_call(
        _ln_body,
        grid=(m // tm,),
        in_specs=[row, row, par, par],
        out_specs=[row, row],
        out_shape=[jax.ShapeDtypeStruct((m, d), F32), jax.ShapeDtypeStruct((m, d), BF16)],
        compiler_params=_cparams(1),
        name="add_layer_norm",
    )(x, mix, g3, b3)


def _causal_conv(h, p2, p1, cw, cb, ridx):
    hm1 = jnp.where(ridx < 1, p1, pltpu.roll(h, 1, 0))
    hm2 = jnp.where(ridx < 1, p2, jnp.where(ridx < 2, p1, pltpu.roll(h, 2, 0)))
    return cb + cw[0:1, :] * hm2 + cw[1:2, :] * hm1 + cw[2:3, :] * h


def _geglu(u, g):
    return 0.5 * g * (1.0 + lax.erf(g * (2.0 ** -0.5))) * u


def _conv_geglu_body(hu_ref, hg_ref, bu_ref, bg_ref, cwu_ref, cwg_ref, cbu_ref, cbg_ref,
                     a_ref, su_ref, sg_ref):
    rows = hu_ref.shape[0]
    ridx = lax.broadcasted_iota(jnp.int32, hu_ref.shape, 0)

    def conv(h_ref, buf_ref, cw_ref, cb_ref, s_ref):
        h = h_ref[...]
        buf = buf_ref[...]
        s_ref[...] = h[rows - 2:rows, :]
        return _causal_conv(h, buf[0:1, :], buf[1:2, :], cw_ref[...], cb_ref[...], ridx)

    u = conv(hu_ref, bu_ref, cwu_ref, cbu_ref, su_ref)
    g = conv(hg_ref, bg_ref, cwg_ref, cbg_ref, sg_ref)
    a_ref[...] = _geglu(u, g).astype(a_ref.dtype)


def conv_geglu(h, buf, conv_w, conv_b, layer, seq, *, tn=512):
    m = h.shape[0]
    bsz = m // seq
    nj = D_FF // tn
    cb3 = conv_b.reshape(conv_b.shape[0], 1, 2 * D_FF)
    hu = pl.BlockSpec((seq, tn), lambda b, j: (b, j))
    hg = pl.BlockSpec((seq, tn), lambda b, j: (b, j + nj))
    bu = pl.BlockSpec((None, None, CONV_W - 1, tn), lambda b, j: (layer, b, 0, j))
    bg = pl.BlockSpec((None, None, CONV_W - 1, tn), lambda b, j: (layer, b, 0, j + nj))
    cwu = pl.BlockSpec((None, CONV_W, tn), lambda b, j: (layer, 0, j))
    cwg = pl.BlockSpec((None, CONV_W, tn), lambda b, j: (layer, 0, j + nj))
    cbu = pl.BlockSpec((None, 1, tn), lambda b, j: (layer, 0, j))
    cbg = pl.BlockSpec((None, 1, tn), lambda b, j: (layer, 0, j + nj))
    st = pl.BlockSpec((None, CONV_W - 1, tn), lambda b, j: (b, 0, j))
    a, su, sg = pl.pallas_call(
        _conv_geglu_body,
        grid=(bsz, nj),
        in_specs=[hu, hg, bu, bg, cwu, cwg, cbu, cbg],
        out_specs=[pl.BlockSpec((seq, tn), lambda b, j: (b, j)), st, st],
        out_shape=[jax.ShapeDtypeStruct((m, D_FF), BF16),
                   jax.ShapeDtypeStruct((bsz, CONV_W - 1, D_FF), F32),
                   jax.ShapeDtypeStruct((bsz, CONV_W - 1, D_FF), F32)],
        compiler_params=_cparams(2),
        name="conv_geglu",
    )(h, h, buf, buf, conv_w, conv_w, cb3, cb3)
    return a, jnp.concatenate([su, sg], axis=-1)


def _ffn_up_fresh_body(x_ref, wu_ref, wg_ref, cwu_ref, cwg_ref, cbu_ref, cbg_ref,
                       a_ref, su_ref, sg_ref, wub_ref, wgb_ref, pu_ref, pg_ref):
    b = pl.program_id(1)
    i = pl.program_id(2)

    @pl.when((b == 0) & (i == 0))
    def _():
        wub_ref[...] = wu_ref[...].astype(BF16)
        wgb_ref[...] = wg_ref[...].astype(BF16)

    @pl.when(i == 0)
    def _():
        pu_ref[...] = jnp.zeros_like(pu_ref)
        pg_ref[...] = jnp.zeros_like(pg_ref)

    x = x_ref[...]
    rows = x.shape[0]
    ridx = lax.broadcasted_iota(jnp.int32, (rows, wub_ref.shape[1]), 0)

    def half(wb_ref, prev_ref, cw_ref, cb_ref, s_ref):
        h = jnp.dot(x, wb_ref[...], preferred_element_type=F32)
        prev = prev_ref[...]
        out = _causal_conv(h, prev[SUBLANES - 2:SUBLANES - 1, :], prev[SUBLANES - 1:SUBLANES, :],
                           cw_ref[...], cb_ref[...], ridx)
        prev_ref[...] = h[rows - SUBLANES:rows, :]
        s_ref[...] = h[rows - 2:rows, :]
        return out

    u = half(wub_ref, pu_ref, cwu_ref, cbu_ref, su_ref)
    g = half(wgb_ref, pg_ref, cwg_ref, cbg_ref, sg_ref)
    a_ref[...] = _geglu(u, g).astype(a_ref.dtype)


def ffn_up_fresh(xb, w_up, conv_w, conv_b, layer, seq, *, tm=1024, tn=512):
    m, k = xb.shape
    bsz = m // seq
    tm = min(tm, seq)
    assert seq % tm == 0 and D_FF % tn == 0
    ni = seq // tm
    nj = D_FF // tn
    cb3 = conv_b.reshape(conv_b.shape[0], 1, 2 * D_FF)
    st = pl.BlockSpec((None, CONV_W - 1, tn), lambda j, b, i: (b, 0, j))
    a, su, sg = pl.pallas_call(
        _ffn_up_fresh_body,
        grid=(nj, bsz, ni),
        in_specs=[pl.BlockSpec((tm, k), lambda j, b, i: (b * ni + i, 0)),
                  pl.BlockSpec((None, k, tn), lambda j, b, i: (layer, 0, j)),
                  pl.BlockSpec((None, k, tn), lambda j, b, i: (layer, 0, j + nj)),
                  pl.BlockSpec((None, CONV_W, tn), lambda j, b, i: (layer, 0, j)),
                  pl.BlockSpec((None, CONV_W, tn), lambda j, b, i: (layer, 0, j + nj)),
                  pl.BlockSpec((None, 1, tn), lambda j, b, i: (layer, 0, j)),
                  pl.BlockSpec((None, 1, tn), lambda j, b, i: (layer, 0, j + nj))],
        out_specs=[pl.BlockSpec((tm, tn), lambda j, b, i: (b * ni + i, j)), st, st],
        out_shape=[jax.ShapeDtypeStruct((m, D_FF), BF16),
                   jax.ShapeDtypeStruct((bsz, CONV_W - 1, D_FF), F32),
                   jax.ShapeDtypeStruct((bsz, CONV_W - 1, D_FF), F32)],
        scratch_shapes=[pltpu.VMEM((k, tn), BF16), pltpu.VMEM((k, tn), BF16),
                        pltpu.VMEM((SUBLANES, tn), F32), pltpu.VMEM((SUBLANES, tn), F32)],
        compiler_params=_cparams(3),
        name="ffn_up_fresh",
    )(xb, w_up, w_up, conv_w, conv_w, cb3, cb3)
    return a, jnp.concatenate([su, sg], axis=-1)


_NT = (((1,), (1,)), ((), ()))
_TN = (((0,), (0,)), ((), ()))


def _hgrn_body(q_ref, f_ref, i_ref, g_ref, lb_ref, ng_ref, s0_ref, tb_ref,
               o_ref, s_ref, st_ref, *, valid):
    seq = q_ref.shape[0]
    slab = tb_ref.shape[1]
    c = HGRN_CHUNK
    heads = range(HEAD_GROUP)
    cols = [slice(hd * DK_A, (hd + 1) * DK_A) for hd in heads]
    tb = tb_ref[...]
    causal = tb[0:slab, :] > 0
    for hd in heads:
        st_ref[hd] = s0_ref[hd].T

    def slab_step(si, carry):
        r0 = pl.multiple_of(si * slab, slab)
        rows = pl.ds(r0, slab)
        q, k, parts = [], [], []
        for hd in heads:
            lb = lb_ref[:, cols[hd]]
            log_lb = jnp.log(lb)
            qa = q_ref[rows, cols[hd]]
            fa = f_ref[rows, cols[hd]]
            x2 = jnp.log1p(-lb) + (jnp.minimum(fa, 0.0) - jnp.log1p(jnp.exp(-jnp.abs(fa))))
            mx = jnp.maximum(log_lb, x2)
            log_f = mx + jnp.log(jnp.exp(log_lb - mx) + jnp.exp(x2 - mx))
            kh = (1.0 - lb) * _sigmoid(-fa)
            if valid < seq:
                live = (r0 + lax.broadcasted_iota(jnp.int32, fa.shape, 0)) < valid
                log_f = jnp.where(live, log_f, 0.0)
                kh = jnp.where(live, kh, 0.0)
            q.append(qa * _sigmoid(qa))
            k.append(kh)
            parts.append(_split3(log_f))
        gg = [sum(jnp.dot(tb, p, preferred_element_type=F32) for p in parts[hd]) for hd in heads]
        qg, kinv, kg, dec, vb = [], [], [], [], []
        for hd in heads:
            g = gg[hd][0:slab, :]
            gl = gg[hd][slab:2 * slab, :]
            qg.append((q[hd] * jnp.exp(g)).astype(BF16))
            kinv.append((k[hd] * jnp.exp(-g)).astype(BF16))
            kg.append((k[hd] * jnp.exp(gl - g)).astype(BF16))
            dec.append(jnp.exp(gl))
            vb.append(i_ref[rows, cols[hd]].astype(BF16))
        att = [lax.dot_general(qg[hd], kinv[hd], _NT, preferred_element_type=F32) for hd in heads]
        o_in = [jnp.dot(jnp.where(causal, att[hd], 0.0).astype(BF16), vb[hd],
                        preferred_element_type=F32) for hd in heads]
        st = [st_ref[hd] for hd in heads]
        outs = [[] for _ in heads]
        for ci in range(slab // c):
            rs = slice(ci * c, (ci + 1) * c)
            for hd in heads:
                outs[hd].append(o_in[hd][rs] + lax.dot_general(
                    qg[hd][rs], st[hd].astype(BF16), _NT, preferred_element_type=F32))
                u = lax.dot_general(vb[hd][rs], kg[hd][rs], _TN, preferred_element_type=F32)
                st[hd] = st[hd] * dec[hd][ci * c:ci * c + 1, :] + u
        for hd in heads:
            st_ref[hd] = st[hd]
            o = jnp.concatenate(outs[hd], axis=0) if len(outs[hd]) > 1 else outs[hd][0]
            o = o * lax.rsqrt(jnp.mean(o * o, axis=-1, keepdims=True) + RMS_EPS) * ng_ref[:, cols[hd]]
            ga = g_ref[rows, cols[hd]]
            o_ref[rows, cols[hd]] = (o * (ga * _sigmoid(ga))).astype(o_ref.dtype)
        return carry

    lax.fori_loop(0, seq // slab, slab_step, 0)
    for hd in heads:
        s_ref[hd] = st_ref[hd].T


def hgrn_mixer(pa, lb, norm_g, s0, layer, seq, valid):
    m = pa.shape[0]
    bsz = m // seq
    slab = min(LANES, seq)
    hg = HEAD_GROUP
    assert seq % slab == 0 and slab % HGRN_CHUNK == 0 and H_A % hg == 0
    if s0 is None:
        s0 = jnp.zeros((1, bsz, H_A, DK_A, DV_A), F32)
        slayer = 0
    else:
        slayer = layer
    r = np.arange(slab)
    same = (r[:, None] // HGRN_CHUNK) == (r[None, :] // HGRN_CHUNK)
    tb = jnp.asarray(np.concatenate([same & (r[None, :] <= r[:, None]), same], axis=0), BF16)
    ngrp = H_A // hg
    col = lambda off: pl.BlockSpec((seq, hg * DK_A), lambda b, h: (b, h + off * ngrp))
    vec = pl.BlockSpec((1, hg * DK_A), lambda b, h: (0, h))
    const = pl.BlockSpec((2 * slab, slab), lambda b, h: (0, 0))
    o, s = pl.pallas_call(
        functools.partial(_hgrn_body, valid=valid),
        grid=(bsz, ngrp),
        in_specs=[col(0), col(1), col(2), col(3), vec, vec,
                  pl.BlockSpec((None, None, hg, DK_A, DV_A), lambda b, h: (slayer, b, h, 0, 0)),
                  const],
        out_specs=[pl.BlockSpec((seq, hg * DV_A), lambda b, h: (b, h)),
                   pl.BlockSpec((None, hg, DK_A, DV_A), lambda b, h: (b, h, 0, 0))],
        out_shape=[jax.ShapeDtypeStruct((m, D_A), BF16),
                   jax.ShapeDtypeStruct((bsz, H_A, DK_A, DV_A), F32)],
        scratch_shapes=[pltpu.VMEM((hg, DV_A, DK_A), F32)],
        compiler_params=_cparams(2),
        name="hgrn_mixer",
    )(pa, pa, pa, pa, lb.reshape(1, D_A), norm_g.reshape(1, D_A), s0, tb)
    return o, s


LORA_IN = 3 * LANES
_LORA_USED = R_DECAY + R_AAA + R_GATE


def _seg_sum(x, sel):
    hi, lo = _split2(x)
    return (jnp.dot(hi, sel, preferred_element_type=F32)
            + jnp.dot(lo, sel, preferred_element_type=F32))


def _rwkv_prep_body(pb_ref, prev_ref, tail_ref, tprev_ref, mu_ref, mut_ref, w0_ref, a0_ref,
                    lora_ref, kk_ref, ka_ref, rk_ref, e_ref, sel_ref,
                    w_o, k_o, v_o, a_o, b_o, wr_o, br_o, kr_o, bonus_o, gate_o):
    pb = pb_ref[:, 0:3 * D_B]
    pm = pb + (prev_ref[:, 0:3 * D_B] - pb) * mu_ref[...]
    r = pm[:, 0:D_B]
    kb = pm[:, D_B:2 * D_B]
    v = pm[:, 2 * D_B:3 * D_B]
    tl = tail_ref[...]
    tm_ = tl + (tprev_ref[...] - tl) * mut_ref[...]
    col = lax.broadcasted_iota(jnp.int32, tm_.shape, 1)
    act = jnp.where(col < R_DECAY, jnp.tanh(tm_),
                    jnp.where(col < R_DECAY + R_AAA, tm_, _sigmoid(tm_)))
    act = jnp.where(col < _LORA_USED, act, 0.0).astype(BF16)
    up = jnp.dot(act, lora_ref[...], preferred_element_type=F32)
    w = w0_ref[...] + up[:, 0:D_B]
    decay = jnp.exp(-jnp.exp(-_softplus(-w) - 0.5))
    a = _sigmoid(a0_ref[...] + up[:, D_B:2 * D_B])
    gate_o[...] = up[:, 2 * D_B:3 * D_B]
    e = e_ref[...]
    sel = sel_ref[...]
    kk = kb * kk_ref[...]
    kk = kk / jnp.maximum(jnp.sqrt(_seg_sum(kk * kk, e)), 1e-12)
    k2 = kb * (1.0 + (a - 1.0) * ka_ref[...])
    bvec = kk * a
    w_o[...] = decay
    k_o[...] = k2
    v_o[...] = v
    a_o[...] = -kk
    b_o[...] = bvec
    wr_o[...] = decay * r
    br_o[...] = _seg_sum(bvec * r, sel)
    kr_o[...] = _seg_sum(k2 * r, sel)
    bonus_o[...] = _seg_sum(r * k2 * rk_ref[...], e) * v


def rwkv_prep(pb, prev, p, layer, *, tm=256):
    m = pb.shape[0]
    tm = min(tm, m)
    pad = ((0, 0), (0, LORA_IN - _LORA_USED))
    tail = jnp.pad(pb[:, 3 * D_B:], pad)
    tprev = jnp.pad(prev[:, 3 * D_B:], pad)
    mu = p["rwkv_mu"][layer]
    mu_main = mu[:3 * D_B].reshape(1, 3 * D_B)
    mu_tail = jnp.pad(mu[3 * D_B:].reshape(1, -1), pad)
    lora = jnp.zeros((LORA_IN, 3 * D_B), F32)
    lora = lora.at[0:R_DECAY, 0:D_B].set(p["rwkv_w2"][layer])
    lora = lora.at[R_DECAY:R_DECAY + R_AAA, D_B:2 * D_B].set(p["rwkv_a2"][layer])
    lora = lora.at[R_DECAY + R_AAA:_LORA_USED, 2 * D_B:].set(p["rwkv_g2"][layer]).astype(BF16)
    c = np.arange(D_B)
    e = jnp.asarray((c[:, None] // HD_B) == (c[None, :] // HD_B), BF16)
    sel = jnp.asarray((c[:, None] // HD_B) == np.arange(LANES)[None, :], BF16)
    vec = lambda name: p[name][layer].reshape(1, D_B)
    row = lambda n: pl.BlockSpec((tm, n), lambda i: (i, 0))
    full = lambda a, b: pl.BlockSpec((a, b), lambda i: (0, 0))
    big = jax.ShapeDtypeStruct((m, D_B), F32)
    small = jax.ShapeDtypeStruct((m, LANES), F32)
    return pl.pallas_call(
        _rwkv_prep_body,
        grid=(m // tm,),
        in_specs=[row(B_PROJ), row(B_PROJ), row(LORA_IN), row(LORA_IN), full(1, 3 * D_B),
                  full(1, LORA_IN), full(1, D_B), full(1, D_B), full(LORA_IN, 3 * D_B),
                  full(1, D_B), full(1, D_B), full(1, D_B), full(D_B, D_B), full(D_B, LANES)],
        out_specs=[row(D_B)] * 6 + [row(LANES)] * 2 + [row(D_B)] * 2,
        out_shape=[big] * 6 + [small] * 2 + [big] * 2,
        compiler_params=_cparams(1),
        name="rwkv_prep",
    )(pb, prev, tail, tprev, mu_main, mu_tail, vec("rwkv_w0"), vec("rwkv_a0"), lora,
      vec("rwkv_k_k"), vec("rwkv_k_a"), p["rwkv_r_k"][layer].reshape(1, D_B), e, sel)


def _rwkv_scan_body(w_ref, k_ref, a_ref, b_ref, wr_ref, v_ref, sc_ref, s0_ref, y_ref, sout_ref,
                    s_scr, *, nch):
    tb = w_ref.shape[0]
    nk = s_scr.shape[0]

    @pl.when(pl.program_id(0) == 0)
    def _():
        s_scr[...] = s0_ref[...]

    def all_parts(x):
        shift = LANES // 2
        while shift >= nch:
            x = x + pltpu.roll(x, shift, 1)
            shift //= 2
        return x

    def step(t, carry):
        sa = None
        yp = None
        for kx in range(nk):
            sk = s_scr[kx]
            pa = sk * a_ref[t, kx:kx + 1, :]
            py = sk * wr_ref[t, kx:kx + 1, :]
            sa = pa if sa is None else sa + pa
            yp = py if yp is None else yp + py
        sa = all_parts(sa)
        yp = all_parts(yp)
        vt = v_ref[t]
        for kx in range(nk):
            s_scr[kx] = (s_scr[kx] * w_ref[t, kx:kx + 1, :] + sa * b_ref[t, kx:kx + 1, :]
                         + vt * k_ref[t, kx:kx + 1, :])
        y_ref[t] = yp + sa * sc_ref[t, 0:1, :] + vt * sc_ref[t, 1:2, :]
        return carry

    lax.fori_loop(0, tb, step, 0)

    @pl.when(pl.program_id(0) == pl.num_programs(0) - 1)
    def _():
        sout_ref[...] = s_scr[...]


def rwkv_scan(w, k, a, b, wr, v, sc, s0, nch, *, tb=16):
    seq, ks, _ = w.shape
    tb = min(tb, seq)
    assert seq % tb == 0
    kblk = pl.BlockSpec((tb, ks, LANES), lambda i: (i, 0, 0))
    vblk = pl.BlockSpec((tb, HD_B, LANES), lambda i: (i, 0, 0))
    sblk = pl.BlockSpec((ks, HD_B, LANES), lambda i: (0, 0, 0))
    return pl.pallas_call(
        functools.partial(_rwkv_scan_body, nch=nch),
        grid=(seq // tb,),
        in_specs=[kblk] * 5 + [vblk, pl.BlockSpec((tb, 2, LANES), lambda i: (i, 0, 0)), sblk],
        out_specs=[vblk, sblk],
        out_shape=[jax.ShapeDtypeStruct((seq, HD_B, LANES), F32),
                   jax.ShapeDtypeStruct((ks, HD_B, LANES), F32)],
        scratch_shapes=[pltpu.VMEM((ks, HD_B, LANES), F32)],
        compiler_params=_cparams(1),
        name="rwkv_scan",
    )(w, k, a, b, wr, v, sc, s0)


def _rwkv_post_body(y_ref, bonus_ref, gate_ref, g_ref, b_ref, e_ref, o_ref):
    e = e_ref[...]
    y = y_ref[...]
    yc = y - _seg_sum(y, e) * (1.0 / HD_B)
    var = _seg_sum(yc * yc, e) * (1.0 / HD_B)
    yn = yc * lax.rsqrt(var + RWKV_GN_EPS) * g_ref[...] + b_ref[...]
    o_ref[...] = ((yn + bonus_ref[...]) * gate_ref[...]).astype(o_ref.dtype)


def rwkv_post(y, bonus, gate, gn_g, gn_b, *, tm=512):
    m = y.shape[0]
    tm = min(tm, m)
    c = np.arange(D_B)
    e = jnp.asarray((c[:, None] // HD_B) == (c[None, :] // HD_B), BF16)
    row = pl.BlockSpec((tm, D_B), lambda i: (i, 0))
    vec = pl.BlockSpec((1, D_B), lambda i: (0, 0))
    return pl.pallas_call(
        _rwkv_post_body,
        grid=(m // tm,),
        in_specs=[row, row, row, vec, vec, pl.BlockSpec((D_B, D_B), lambda i: (0, 0))],
        out_specs=row,
        out_shape=jax.ShapeDtypeStruct((m, D_B), BF16),
        compiler_params=_cparams(1),
        name="rwkv_post",
    )(y, bonus, gate, gn_g.reshape(1, D_B), gn_b.reshape(1, D_B), e)


def rwkv_mixer(pb, shift, s0, p, layer, seq):
    m = pb.shape[0]
    bsz = m // seq
    nch = bsz * H_B
    nsplit = LANES // nch
    ks = HD_B // nsplit
    assert nch * nsplit == LANES
    pb3 = pb.reshape(bsz, seq, B_PROJ)
    first = jnp.zeros((bsz, 1, B_PROJ), F32) if shift is None else shift[:, None]
    prev = jnp.concatenate([first, pb3[:, :-1]], axis=1).reshape(m, B_PROJ)
    w, k, v, a, b, wr, br, kr, bonus, gate = rwkv_prep(pb, prev, p, layer)

    def key_major(x):
        x = x.reshape(bsz, seq, H_B, nsplit, ks).transpose(1, 4, 3, 0, 2)
        return x.reshape(seq, ks, LANES)

    def per_chain(x):
        x = jnp.moveaxis(x.reshape((bsz, seq) + x.shape[1:]), 0, -2)
        x = x.reshape(x.shape[:-2] + (nch,))
        return jnp.tile(x, (1,) * (x.ndim - 1) + (nsplit,))

    v_c = per_chain(v.reshape(m, H_B, HD_B).transpose(0, 2, 1))
    sc = per_chain(jnp.stack([br[:, :H_B], kr[:, :H_B]], axis=1))
    if s0 is None:
        s0_c = jnp.zeros((ks, HD_B, LANES), F32)
    else:
        s0_c = s0.reshape(bsz, H_B, HD_B, nsplit, ks).transpose(4, 2, 3, 0, 1).reshape(ks, HD_B, LANES)
    y_c, s_c = rwkv_scan(key_major(w), key_major(k), key_major(a), key_major(b), key_major(wr),
                         v_c, sc, s0_c, nch)
    y = y_c[:, :, :nch].reshape(seq, HD_B, bsz, H_B).transpose(2, 0, 3, 1).reshape(m, D_B)
    s_new = s_c.reshape(ks, HD_B, nsplit, bsz, H_B).transpose(3, 4, 1, 2, 0).reshape(bsz, H_B, HD_B, HD_B)
    o = rwkv_post(y, bonus, gate, p["rwkv_gn_g"][layer], p["rwkv_gn_b"][layer])
    return o, s_new


def even_mixer(xb, seq, s_hgrn, s_rwkv, s_shift, lbs, p, e):
    m = xb.shape[0]
    bsz = m // seq
    pa = matmul(xb, p["w_in_even"], e, n_out=4 * D_A)
    pb = matmul(xb, p["w_in_even"], e, n_out=B_PROJ, col_off=4 * D_A)
    seq_a = -(-seq // HGRN_CHUNK) * HGRN_CHUNK
    pa_p = pa
    if seq_a != seq:
        pa_p = jnp.pad(pa.reshape(bsz, seq, -1), ((0, 0), (0, seq_a - seq), (0, 0))).reshape(bsz * seq_a, -1)
    o_a, h_new = hgrn_mixer(pa_p, lbs[e], p["hgrn_norm_g"][e], s_hgrn, e, seq_a, seq)
    if seq_a != seq:
        o_a = o_a.reshape(bsz, seq_a, D_A)[:, :seq].reshape(m, D_A)
    o_b, r_new = rwkv_mixer(pb, None if s_shift is None else s_shift[e],
                            None if s_rwkv is None else s_rwkv[e], p, e, seq)
    mix = matmul(jnp.concatenate([o_a, o_b], axis=-1), p["w_out_even"], e)
    return mix, h_new, r_new, pb.reshape(bsz, seq, B_PROJ)[:, -1]


SB_TQ = 256
SB_TK = 128
SB_SCALE = HD_C ** -0.5


def _sb_block_sums(sp, lt2):
    hi, lo = _split2(sp)
    return (jnp.dot(hi, lt2, preferred_element_type=F32)
            + jnp.dot(lo, lt2, preferred_element_type=F32))


def _sb_prompt_body(bias_ref, q_ref, k_ref, v_ref, lt2_ref, o_ref, qs_ref, acc_ref, c_ref, *,
                    layer):
    tq, tk, hg = SB_TQ, SB_TK, HEAD_GROUP
    nd = tq // tk
    qi = pl.program_id(2)
    lt2 = lt2_ref[...]
    row = lax.broadcasted_iota(jnp.int32, (tq, tk), 0)
    lane = lax.broadcasted_iota(jnp.int32, (tq, tk), 1)
    qs_ref[...] = (q_ref[...] * SB_SCALE).astype(BF16)
    acc_ref[...] = jnp.zeros_like(acc_ref)
    c_ref[...] = jnp.zeros_like(c_ref)
    heads = range(hg)
    cols = [slice(hd * HD_C, (hd + 1) * HD_C) for hd in heads]
    biases = [bias_ref[layer, pl.program_id(1) * hg + hd] for hd in heads]

    def tile(k0, mask):
        keys = pl.ds(k0, tk)
        z = [lax.dot_general(qs_ref[:, cols[hd]], k_ref[keys, cols[hd]].astype(BF16), _NT,
                             preferred_element_type=F32) + biases[hd] for hd in heads]
        sp = [_softplus(z[hd]) for hd in heads]
        if mask is not None:
            sp = [jnp.where(mask, x, 0.0) for x in sp]
        sums = [_sb_block_sums(sp[hd], lt2) for hd in heads]
        for hd in heads:
            att = jnp.exp(z[hd] - sums[hd][:, 0:tk] - c_ref[hd])
            if mask is not None:
                att = jnp.where(mask, att, 0.0)
            acc_ref[:, cols[hd]] += jnp.dot(att.astype(BF16), v_ref[keys, cols[hd]].astype(BF16),
                                            preferred_element_type=F32)
        for hd in heads:
            c_ref[hd] += sums[hd][:, tk:2 * tk]

    for d in reversed(range(nd)):
        tile(pl.multiple_of(qi * tq + d * tk, tk), (d * tk + lane) < row)

    def full_tile(jj, carry):
        tile(pl.multiple_of((qi * nd - 1 - jj) * tk, tk), None)
        return carry

    lax.fori_loop(0, qi * nd, full_tile, 0)
    o_ref[...] = acc_ref[...].astype(o_ref.dtype)


def _lt2(tk):
    r = np.arange(tk)
    return jnp.asarray(np.concatenate([r[:, None] >= r[None, :], np.ones((tk, tk), bool)], axis=1),
                       BF16)


def sb_prompt(qkv, bias, layer, seq):
    m = qkv.shape[0]
    bsz = m // seq
    tq, hg = SB_TQ, HEAD_GROUP
    assert seq % tq == 0 and H_C % hg == 0
    nq = seq // tq
    ngrp = H_C // hg
    wide = hg * HD_C
    return pl.pallas_call(
        functools.partial(_sb_prompt_body, layer=layer),
        grid=(bsz, ngrp, nq),
        in_specs=[pl.BlockSpec(memory_space=pltpu.SMEM),
                  pl.BlockSpec((tq, wide), lambda b, h, i: (b * nq + i, h)),
                  pl.BlockSpec((seq, wide), lambda b, h, i: (b, ngrp + h)),
                  pl.BlockSpec((seq, wide), lambda b, h, i: (b, 2 * ngrp + h)),
                  pl.BlockSpec((SB_TK, 2 * SB_TK), lambda b, h, i: (0, 0))],
        out_specs=pl.BlockSpec((tq, wide), lambda b, h, i: (b * nq + i, h)),
        out_shape=jax.ShapeDtypeStruct((m, D_MODEL), BF16),
        scratch_shapes=[pltpu.VMEM((tq, wide), BF16), pltpu.VMEM((tq, wide), F32),
                        pltpu.VMEM((hg, tq, SB_TK), F32)],
        compiler_params=_cparams(3),
        name="sb_prompt",
    )(bias, qkv, qkv, qkv, _lt2(SB_TK))


def _sb_sample_body(pt_ref, q_ref, kn_ref, vn_ref, kp_ref, vp_ref, bias_ref, lt2_ref, o_ref,
                    qs_ref, acc_ref, c_ref):
    p = pl.program_id(1)
    nq = q_ref.shape[0]
    lt2 = lt2_ref[...]
    tk = PAGE_SIZE

    def block(k_ref, v_ref, new_keys):
        zs = []
        for h in range(H_C):
            cols = slice(h * HD_C, (h + 1) * HD_C)
            kh = k_ref[pl.ds(h, PAGE_SIZE, stride=H_C), :].astype(BF16)
            zs.append(lax.dot_general(qs_ref[:, cols].astype(BF16), kh, _NT,
                                      preferred_element_type=F32))
        z = jnp.concatenate(zs, axis=0) + bias_ref[...]
        sp = _softplus(z)
        if new_keys:
            s_idx = lax.broadcasted_iota(jnp.int32, z.shape, 1)
            q_idx = lax.broadcasted_iota(jnp.int32, z.shape, 0) % nq
            mask = s_idx < q_idx
            sp = jnp.where(mask, sp, 0.0)
        sums = _sb_block_sums(sp, lt2)
        att = jnp.exp(z - sums[:, 0:tk] - c_ref[...])
        if new_keys:
            att = jnp.where(mask, att, 0.0)
        for h in range(H_C):
            cols = slice(h * HD_C, (h + 1) * HD_C)
            vh = v_ref[pl.ds(h, PAGE_SIZE, stride=H_C), :].astype(BF16)
            acc_ref[:, cols] += jnp.dot(att[h * nq:(h + 1) * nq, :].astype(BF16), vh,
                                        preferred_element_type=F32)
        c_ref[...] += sums[:, tk:2 * tk]

    @pl.when(p == 0)
    def _():
        qs_ref[...] = q_ref[...] * SB_SCALE
        acc_ref[...] = jnp.zeros_like(acc_ref)
        c_ref[...] = jnp.zeros_like(c_ref)
        block(kn_ref, vn_ref, True)

    @pl.when(p > 0)
    def _():
        block(kp_ref, vp_ref, False)

    @pl.when(p == pl.num_programs(1) - 1)
    def _():
        o_ref[...] = acc_ref[...].astype(o_ref.dtype)


def sb_sample(qkv, cache_k, cache_v, page_table, bias, layer, seq):
    m = qkv.shape[0]
    bsz = m // seq
    n_pages = page_table.shape[1]
    assert seq * H_C == LANES and seq <= PAGE_SIZE
    rows = PAGE_SIZE * H_C
    ck = cache_k.reshape(cache_k.shape[0], cache_k.shape[1], rows, HD_C)
    cv = cache_v.reshape(cache_v.shape[0], cache_v.shape[1], rows, HD_C)
    pad = ((0, 0), (0, (PAGE_SIZE - seq) * H_C), (0, 0))
    kn = jnp.pad(qkv[:, D_MODEL:2 * D_MODEL].reshape(bsz, seq * H_C, HD_C), pad)
    vn = jnp.pad(qkv[:, 2 * D_MODEL:].reshape(bsz, seq * H_C, HD_C), pad)
    bias_rows = jnp.broadcast_to(jnp.repeat(bias[layer], seq)[:, None], (LANES, PAGE_SIZE))
    new = pl.BlockSpec((None, rows, HD_C), lambda b, p, pt: (b, 0, 0))
    page = pl.BlockSpec((None, None, rows, HD_C),
                        lambda b, p, pt: (layer, pt[b, n_pages - jnp.maximum(p, 1)], 0, 0))
    const = pl.BlockSpec((PAGE_SIZE, PAGE_SIZE), lambda b, p, pt: (0, 0))
    grid_spec = pltpu.PrefetchScalarGridSpec(
        num_scalar_prefetch=1,
        grid=(bsz, n_pages + 1),
        in_specs=[pl.BlockSpec((seq, D_MODEL), lambda b, p, pt: (b, 0)), new, new, page, page,
                  const, pl.BlockSpec((PAGE_SIZE, 2 * PAGE_SIZE), lambda b, p, pt: (0, 0))],
        out_specs=pl.BlockSpec((seq, D_MODEL), lambda b, p, pt: (b, 0)),
        scratch_shapes=[pltpu.VMEM((seq, D_MODEL), F32), pltpu.VMEM((seq, D_MODEL), F32),
                        pltpu.VMEM((LANES, PAGE_SIZE), F32)],
    )
    return pl.pallas_call(
        _sb_sample_body,
        grid_spec=grid_spec,
        out_shape=jax.ShapeDtypeStruct((m, D_MODEL), BF16),
        compiler_params=_cparams(2),
        name="sb_sample",
    )(page_table, qkv, kn, vn, ck, cv, bias_rows, _lt2(PAGE_SIZE))


def sb_mixer(xb, seq, cache_k, cache_v, page_table, p, o):
    m = xb.shape[0]
    bsz = m // seq
    qkv = matmul(xb, p["w_qkv_odd"], o)
    if cache_k is None:
        att = sb_prompt(qkv, p["sb_bias"], o, seq)
    else:
        att = sb_sample(qkv, cache_k, cache_v, page_table, p["sb_bias"], o, seq)
    mix = matmul(att, p["w_out_odd"], o)
    k_new = qkv[:, D_MODEL:2 * D_MODEL].reshape(bsz, seq, H_C, HD_C)
    v_new = qkv[:, 2 * D_MODEL:].reshape(bsz, seq, H_C, HD_C)
    return mix, k_new, v_new


def conv_ffn(xb, seq, buf, p, layer):
    if buf is None:
        a, buf_new = ffn_up_fresh(xb, p["ffn_up"], p["ffn_conv_w"], p["ffn_conv_b"], layer, seq)
    else:
        h = matmul(xb, p["ffn_up"], layer)
        a, buf_new = conv_geglu(h, buf, p["ffn_conv_w"], p["ffn_conv_b"], layer, seq)
    return matmul(a, p["ffn_down"], layer, tm=512), buf_new


def _hgrn_lower_bounds(lb_param):
    lbs = jnp.cumsum(jax.nn.softmax(lb_param.astype(F32), axis=0), axis=0)
    return lbs - lbs[0:1]


def kernel(x_prompt, x_sample, cache_k, cache_v, page_table, state_hgrn, state_rwkv, state_rwkv_shift, state_ffn_conv, w_in_even, hgrn_lb_param, hgrn_norm_g, rwkv_mu, rwkv_w0, rwkv_w2, rwkv_a0, rwkv_a2, rwkv_g2, rwkv_k_k, rwkv_k_a, rwkv_r_k, rwkv_gn_g, rwkv_gn_b, w_out_even, w_qkv_odd, w_out_odd, sb_bias, ln1_g, ln1_b, ln2_g, ln2_b, ffn_up, ffn_conv_w, ffn_conv_b, ffn_down):
    p = dict(w_in_even=w_in_even, hgrn_norm_g=hgrn_norm_g, rwkv_mu=rwkv_mu, rwkv_w0=rwkv_w0,
             rwkv_w2=rwkv_w2, rwkv_a0=rwkv_a0, rwkv_a2=rwkv_a2, rwkv_g2=rwkv_g2, rwkv_k_k=rwkv_k_k,
             rwkv_k_a=rwkv_k_a, rwkv_r_k=rwkv_r_k, rwkv_gn_g=rwkv_gn_g, rwkv_gn_b=rwkv_gn_b,
             w_out_even=w_out_even, w_qkv_odd=w_qkv_odd, w_out_odd=w_out_odd, sb_bias=sb_bias,
             ffn_up=ffn_up, ffn_conv_w=ffn_conv_w, ffn_conv_b=ffn_conv_b, ffn_down=ffn_down)
    lbs = _hgrn_lower_bounds(hgrn_lb_param)
    bp, lp, _ = x_prompt.shape
    bs, ls, _ = x_sample.shape
    groups = [
        dict(x=x_prompt.reshape(bp * lp, D_MODEL), seq=lp, sample=False),
        dict(x=x_sample.reshape(bs * ls, D_MODEL), seq=ls, sample=True),
    ]
    for g in groups:
        g["xb"] = g["x"].astype(BF16)
        g["k"], g["v"], g["hgrn"], g["rwkv"], g["shift"], g["conv"] = [], [], [], [], [], []
    for layer in range(DEPTH):
        for g in groups:
            seq, sample = g["seq"], g["sample"]
            if layer % 2 == 0:
                e = layer // 2
                mix, h_new, r_new, sh_new = even_mixer(
                    g["xb"], seq, state_hgrn if sample else None, state_rwkv if sample else None,
                    state_rwkv_shift if sample else None, lbs, p, e)
                g["hgrn"].append(h_new)
                g["rwkv"].append(r_new)
                g["shift"].append(sh_new)
            else:
                o = layer // 2
                mix, k_new, v_new = sb_mixer(g["xb"], seq, cache_k if sample else None,
                                             cache_v if sample else None, page_table, p, o)
                g["k"].append(k_new)
                g["v"].append(v_new)
            g["x"], g["xb"] = add_layer_norm(g["x"], mix, ln1_g, ln1_b, layer)
            f, c_new = conv_ffn(g["xb"], seq, state_ffn_conv if sample else None, p, layer)
            g["conv"].append(c_new)
            g["x"], g["xb"] = add_layer_norm(g["x"], f, ln2_g, ln2_b, layer)
    gp, gs = groups
    st = jnp.stack
    return (gp["x"].reshape(bp, lp, D_MODEL), gs["x"].reshape(bs, ls, D_MODEL),
            st(gp["k"]), st(gp["v"]), st(gs["k"]), st(gs["v"]),
            st(gp["hgrn"]), st(gs["hgrn"]), st(gp["rwkv"]), st(gs["rwkv"]),
            st(gp["shift"]), st(gs["shift"]), st(gp["conv"]), st(gs["conv"]))
```

```python
import functools

import jax
import jax.numpy as jnp
import numpy as np
from jax import lax
from jax.experimental import pallas as pl
from jax.experimental.pallas import tpu as pltpu

F32 = jnp.float32
BF16 = jnp.bfloat16

D_MODEL = 2048
DEPTH = 4
PAGE_SIZE = 128
D_A = D_MODEL // 2
DK_A = 128
H_A = D_A // DK_A
DV_A = D_A // H_A
D_B = D_MODEL - D_A
HD_B = 64
H_B = D_B // HD_B
R_DECAY = 64
R_AAA = 64
R_GATE = 160
B_PROJ = 3 * D_B + R_DECAY + R_AAA + R_GATE
RWKV_GN_EPS = 64e-5
HD_C = 128
H_C = D_MODEL // HD_C
D_FF = 5632
CONV_W = 3
ALPHA = (2 * DEPTH) ** 0.25
LN_EPS = 1e-5
RMS_EPS = 1e-6

LANES = 128
SUBLANES = 8
HGRN_CHUNK = 16
HEAD_GROUP = 4
VMEM_LIMIT = 56 * 1024 * 1024


def _cparams(n_axes):
    return pltpu.CompilerParams(dimension_semantics=("arbitrary",) * n_axes,
                                vmem_limit_bytes=VMEM_LIMIT)


def _split2(x):
    hi = x.astype(BF16)
    lo = (x - hi.astype(F32)).astype(BF16)
    return hi, lo


def _split3(x):
    hi = x.astype(BF16)
    r = x - hi.astype(F32)
    mid = r.astype(BF16)
    lo = (r - mid.astype(F32)).astype(BF16)
    return hi, mid, lo


def _softplus(x):
    return jnp.maximum(x, 0.0) + jnp.log1p(jnp.exp(-jnp.abs(x)))


def _sigmoid(x):
    return 1.0 / (1.0 + jnp.exp(-x))


def _mm_body(*refs):
    *x_refs, w_ref, o_ref, wb_ref = refs

    @pl.when(pl.program_id(1) == 0)
    def _():
        wb_ref[...] = w_ref[...].astype(BF16)

    acc = None
    k0 = 0
    for x_ref in x_refs:
        k1 = k0 + x_ref.shape[1]
        part = jnp.dot(x_ref[...].astype(BF16), wb_ref[k0:k1, :], preferred_element_type=F32)
        acc = part if acc is None else acc + part
        k0 = k1
    o_ref[...] = acc.astype(o_ref.dtype)


def matmul(xs, w, layer, *, n_out=None, col_off=0, tm=1024, tn=512, out_dtype=F32):
    xs = xs if isinstance(xs, (list, tuple)) else [xs]
    m = xs[0].shape[0]
    k = sum(x.shape[1] for x in xs)
    n = n_out or w.shape[2]
    tm = min(tm, m)
    tn = min(tn, n)
    assert m % tm == 0 and col_off % tn == 0 and k == w.shape[1]
    joff = col_off // tn
    return pl.pallas_call(
        _mm_body,
        grid=(pl.cdiv(n, tn), m // tm),
        in_specs=[pl.BlockSpec((tm, x.shape[1]), lambda j, i: (i, 0)) for x in xs]
        + [pl.BlockSpec((None, k, tn), lambda j, i: (layer, 0, j + joff))],
        out_specs=pl.BlockSpec((tm, tn), lambda j, i: (i, j)),
        out_shape=jax.ShapeDtypeStruct((m, n), out_dtype),
        scratch_shapes=[pltpu.VMEM((k, tn), BF16)],
        compiler_params=_cparams(2),
        name="matmul",
    )(*xs, w)


def _ln_body(x_ref, m_ref, g_ref, b_ref, o_ref, ob_ref):
    v = ALPHA * x_ref[...] + m_ref[...]
    mu = jnp.mean(v, axis=-1, keepdims=True)
    vc = v - mu
    var = jnp.mean(vc * vc, axis=-1, keepdims=True)
    y = vc * lax.rsqrt(var + LN_EPS) * g_ref[...] + b_ref[...]
    o_ref[...] = y
    ob_ref[...] = y.astype(BF16)


def add_layer_norm(x, mix, g, b, layer, *, tm=512):
    m, d = x.shape
    tm = min(tm, m)
    g3 = g.reshape(g.shape[0], 1, d)
    b3 = b.reshape(b.shape[0], 1, d)
    row = pl.BlockSpec((tm, d), lambda i: (i, 0))
    par = pl.BlockSpec((None, 1, d), lambda i: (layer, 0, 0))
    return pl.pallas_call(
        _ln_body,
        grid=(m // tm,),
        in_specs=[row, row, par, par],
        out_specs=[row, row],
        out_shape=[jax.ShapeDtypeStruct((m, d), F32), jax.ShapeDtypeStruct((m, d), BF16)],
        compiler_params=_cparams(1),
        name="add_layer_norm",
    )(x, mix, g3, b3)


def _causal_conv(h, p2, p1, cw, cb, ridx):
    hm1 = jnp.where(ridx < 1, p1, pltpu.roll(h, 1, 0))
    hm2 = jnp.where(ridx < 1, p2, jnp.where(ridx < 2, p1, pltpu.roll(h, 2, 0)))
    return cb + cw[0:1, :] * hm2 + cw[1:2, :] * hm1 + cw[2:3, :] * h


def _geglu(u, g):
    return 0.5 * g * (1.0 + lax.erf(g * (2.0 ** -0.5))) * u


def _conv_geglu_body(hu_ref, hg_ref, bu_ref, bg_ref, cwu_ref, cwg_ref, cbu_ref, cbg_ref,
                     a_ref, su_ref, sg_ref):
    rows = hu_ref.shape[0]
    ridx = lax.broadcasted_iota(jnp.int32, hu_ref.shape, 0)

    def conv(h_ref, buf_ref, cw_ref, cb_ref, s_ref):
        h = h_ref[...]
        buf = buf_ref[...]
        s_ref[...] = h[rows - 2:rows, :]
        return _causal_conv(h, buf[0:1, :], buf[1:2, :], cw_ref[...], cb_ref[...], ridx)

    u = conv(hu_ref, bu_ref, cwu_ref, cbu_ref, su_ref)
    g = conv(hg_ref, bg_ref, cwg_ref, cbg_ref, sg_ref)
    a_ref[...] = _geglu(u, g).astype(a_ref.dtype)


def conv_geglu(h, buf, conv_w, conv_b, layer, seq, *, tn=512):
    m = h.shape[0]
    bsz = m // seq
    nj = D_FF // tn
    cb3 = conv_b.reshape(conv_b.shape[0], 1, 2 * D_FF)
    hu = pl.BlockSpec((seq, tn), lambda b, j: (b, j))
    hg = pl.BlockSpec((seq, tn), lambda b, j: (b, j + nj))
    bu = pl.BlockSpec((None, None, CONV_W - 1, tn), lambda b, j: (layer, b, 0, j))
    bg = pl.BlockSpec((None, None, CONV_W - 1, tn), lambda b, j: (layer, b, 0, j + nj))
    cwu = pl.BlockSpec((None, CONV_W, tn), lambda b, j: (layer, 0, j))
    cwg = pl.BlockSpec((None, CONV_W, tn), lambda b, j: (layer, 0, j + nj))
    cbu = pl.BlockSpec((None, 1, tn), lambda b, j: (layer, 0, j))
    cbg = pl.BlockSpec((None, 1, tn), lambda b, j: (layer, 0, j + nj))
    st = pl.BlockSpec((None, CONV_W - 1, tn), lambda b, j: (b, 0, j))
    a, su, sg = pl.pallas_call(
        _conv_geglu_body,
        grid=(bsz, nj),
        in_specs=[hu, hg, bu, bg, cwu, cwg, cbu, cbg],
        out_specs=[pl.BlockSpec((seq, tn), lambda b, j: (b, j)), st, st],
        out_shape=[jax.ShapeDtypeStruct((m, D_FF), BF16),
                   jax.ShapeDtypeStruct((bsz, CONV_W - 1, D_FF), F32),
                   jax.ShapeDtypeStruct((bsz, CONV_W - 1, D_FF), F32)],
        compiler_params=_cparams(2),
        name="conv_geglu",
    )(h, h, buf, buf, conv_w, conv_w, cb3, cb3)
    return a, jnp.concatenate([su, sg], axis=-1)


def _ffn_up_fresh_body(x_ref, wu_ref, wg_ref, cwu_ref, cwg_ref, cbu_ref, cbg_ref,
                       a_ref, su_ref, sg_ref, wub_ref, wgb_ref, pu_ref, pg_ref):
    b = pl.program_id(1)
    i = pl.program_id(2)

    @pl.when((b == 0) & (i == 0))
    def _():
        wub_ref[...] = wu_ref[...].astype(BF16)
        wgb_ref[...] = wg_ref[...].astype(BF16)

    @pl.when(i == 0)
    def _():
        pu_ref[...] = jnp.zeros_like(pu_ref)
        pg_ref[...] = jnp.zeros_like(pg_ref)

    x = x_ref[...]
    rows = x.shape[0]
    ridx = lax.broadcasted_iota(jnp.int32, (rows, wub_ref.shape[1]), 0)

    def half(wb_ref, prev_ref, cw_ref, cb_ref, s_ref):
        h = jnp.dot(x, wb_ref[...], preferred_element_type=F32)
        prev = prev_ref[...]
        out = _causal_conv(h, prev[SUBLANES - 2:SUBLANES - 1, :], prev[SUBLANES - 1:SUBLANES, :],
                           cw_ref[...], cb_ref[...], ridx)
        prev_ref[...] = h[rows - SUBLANES:rows, :]
        s_ref[...] = h[rows - 2:rows, :]
        return out

    u = half(wub_ref, pu_ref, cwu_ref, cbu_ref, su_ref)
    g = half(wgb_ref, pg_ref, cwg_ref, cbg_ref, sg_ref)
    a_ref[...] = _geglu(u, g).astype(a_ref.dtype)


def ffn_up_fresh(xb, w_up, conv_w, conv_b, layer, seq, *, tm=1024, tn=512):
    m, k = xb.shape
    bsz = m // seq
    tm = min(tm, seq)
    assert seq % tm == 0 and D_FF % tn == 0
    ni = seq // tm
    nj = D_FF // tn
    cb3 = conv_b.reshape(conv_b.shape[0], 1, 2 * D_FF)
    st = pl.BlockSpec((None, CONV_W - 1, tn), lambda j, b, i: (b, 0, j))
    a, su, sg = pl.pallas_call(
        _ffn_up_fresh_body,
        grid=(nj, bsz, ni),
        in_specs=[pl.BlockSpec((tm, k), lambda j, b, i: (b * ni + i, 0)),
                  pl.BlockSpec((None, k, tn), lambda j, b, i: (layer, 0, j)),
                  pl.BlockSpec((None, k, tn), lambda j, b, i: (layer, 0, j + nj)),
                  pl.BlockSpec((None, CONV_W, tn), lambda j, b, i: (layer, 0, j)),
                  pl.BlockSpec((None, CONV_W, tn), lambda j, b, i: (layer, 0, j + nj)),
                  pl.BlockSpec((None, 1, tn), lambda j, b, i: (layer, 0, j)),
                  pl.BlockSpec((None, 1, tn), lambda j, b, i: (layer, 0, j + nj))],
        out_specs=[pl.BlockSpec((tm, tn), lambda j, b, i: (b * ni + i, j)), st, st],
        out_shape=[jax.ShapeDtypeStruct((m, D_FF), BF16),
                   jax.ShapeDtypeStruct((bsz, CONV_W - 1, D_FF), F32),
                   jax.ShapeDtypeStruct((bsz, CONV_W - 1, D_FF), F32)],
        scratch_shapes=[pltpu.VMEM((k, tn), BF16), pltpu.VMEM((k, tn), BF16),
                        pltpu.VMEM((SUBLANES, tn), F32), pltpu.VMEM((SUBLANES, tn), F32)],
        compiler_params=_cparams(3),
        name="ffn_up_fresh",
    )(xb, w_up, w_up, conv_w, conv_w, cb3, cb3)
    return a, jnp.concatenate([su, sg], axis=-1)


_NT = (((1,), (1,)), ((), ()))
_TN = (((0,), (0,)), ((), ()))


def _hgrn_body(q_ref, f_ref, i_ref, g_ref, lb_ref, ng_ref, s0_ref, tb_ref,
               o_ref, s_ref, st_ref, *, valid):
    seq = q_ref.shape[0]
    slab = tb_ref.shape[1]
    c = HGRN_CHUNK
    heads = range(HEAD_GROUP)
    cols = [slice(hd * DK_A, (hd + 1) * DK_A) for hd in heads]
    tb = tb_ref[...]
    causal = tb[0:slab, :] > 0
    for hd in heads:
        st_ref[hd] = s0_ref[hd].T

    def slab_step(si, carry):
        r0 = pl.multiple_of(si * slab, slab)
        rows = pl.ds(r0, slab)
        q, k, parts = [], [], []
        for hd in heads:
            lb = lb_ref[:, cols[hd]]
            log_lb = jnp.log(lb)
            qa = q_ref[rows, cols[hd]]
            fa = f_ref[rows, cols[hd]]
            x2 = jnp.log1p(-lb) + (jnp.minimum(fa, 0.0) - jnp.log1p(jnp.exp(-jnp.abs(fa))))
            mx = jnp.maximum(log_lb, x2)
            log_f = mx + jnp.log(jnp.exp(log_lb - mx) + jnp.exp(x2 - mx))
            kh = (1.0 - lb) * _sigmoid(-fa)
            if valid < seq:
                live = (r0 + lax.broadcasted_iota(jnp.int32, fa.shape, 0)) < valid
                log_f = jnp.where(live, log_f, 0.0)
                kh = jnp.where(live, kh, 0.0)
            q.append(qa * _sigmoid(qa))
            k.append(kh)
            parts.append(_split3(log_f))
        gg = [sum(jnp.dot(tb, p, preferred_element_type=F32) for p in parts[hd]) for hd in heads]
        qg, kinv, kg, dec, vb = [], [], [], [], []
        for hd in heads:
            g = gg[hd][0:slab, :]
            gl = gg[hd][slab:2 * slab, :]
            qg.append((q[hd] * jnp.exp(g)).astype(BF16))
            kinv.append((k[hd] * jnp.exp(-g)).astype(BF16))
            kg.append((k[hd] * jnp.exp(gl - g)).astype(BF16))
            dec.append(jnp.exp(gl))
            vb.append(i_ref[rows, cols[hd]].astype(BF16))
        att = [lax.dot_general(qg[hd], kinv[hd], _NT, preferred_element_type=F32) for hd in heads]
        o_in = [jnp.dot(jnp.where(causal, att[hd], 0.0).astype(BF16), vb[hd],
                        preferred_element_type=F32) for hd in heads]
        st = [st_ref[hd] for hd in heads]
        outs = [[] for _ in heads]
        for ci in range(slab // c):
            rs = slice(ci * c, (ci + 1) * c)
            for hd in heads:
                outs[hd].append(o_in[hd][rs] + lax.dot_general(
                    qg[hd][rs], st[hd].astype(BF16), _NT, preferred_element_type=F32))
                u = lax.dot_general(vb[hd][rs], kg[hd][rs], _TN, preferred_element_type=F32)
                st[hd] = st[hd] * dec[hd][ci * c:ci * c + 1, :] + u
        for hd in heads:
            st_ref[hd] = st[hd]
            o = jnp.concatenate(outs[hd], axis=0) if len(outs[hd]) > 1 else outs[hd][0]
            o = o * lax.rsqrt(jnp.mean(o * o, axis=-1, keepdims=True) + RMS_EPS) * ng_ref[:, cols[hd]]
            ga = g_ref[rows, cols[hd]]
            o_ref[rows, cols[hd]] = (o * (ga * _sigmoid(ga))).astype(o_ref.dtype)
        return carry

    lax.fori_loop(0, seq // slab, slab_step, 0)
    for hd in heads:
        s_ref[hd] = st_ref[hd].T


def hgrn_mixer(pa, lb, norm_g, s0, layer, seq, valid):
    m = pa.shape[0]
    bsz = m // seq
    slab = min(LANES, seq)
    hg = HEAD_GROUP
    assert seq % slab == 0 and slab % HGRN_CHUNK == 0 and H_A % hg == 0
    if s0 is None:
        s0 = jnp.zeros((1, bsz, H_A, DK_A, DV_A), F32)
        slayer = 0
    else:
        slayer = layer
    r = np.arange(slab)
    same = (r[:, None] // HGRN_CHUNK) == (r[None, :] // HGRN_CHUNK)
    tb = jnp.asarray(np.concatenate([same & (r[None, :] <= r[:, None]), same], axis=0), BF16)
    ngrp = H_A // hg
    col = lambda off: pl.BlockSpec((seq, hg * DK_A), lambda b, h: (b, h + off * ngrp))
    vec = pl.BlockSpec((1, hg * DK_A), lambda b, h: (0, h))
    const = pl.BlockSpec((2 * slab, slab), lambda b, h: (0, 0))
    o, s = pl.pallas_call(
        functools.partial(_hgrn_body, valid=valid),
        grid=(bsz, ngrp),
        in_specs=[col(0), col(1), col(2), col(3), vec, vec,
                  pl.BlockSpec((None, None, hg, DK_A, DV_A), lambda b, h: (slayer, b, h, 0, 0)),
                  const],
        out_specs=[pl.BlockSpec((seq, hg * DV_A), lambda b, h: (b, h)),
                   pl.BlockSpec((None, hg, DK_A, DV_A), lambda b, h: (b, h, 0, 0))],
        out_shape=[jax.ShapeDtypeStruct((m, D_A), BF16),
                   jax.ShapeDtypeStruct((bsz, H_A, DK_A, DV_A), F32)],
        scratch_shapes=[pltpu.VMEM((hg, DV_A, DK_A), F32)],
        compiler_params=_cparams(2),
        name="hgrn_mixer",
    )(pa, pa, pa, pa, lb.reshape(1, D_A), norm_g.reshape(1, D_A), s0, tb)
    return o, s


LORA_IN = 3 * LANES
PB_PAD = 3 * D_B + LORA_IN
_LORA_USED = R_DECAY + R_AAA + R_GATE


def _seg_sum(x, sel):
    hi, lo = _split2(x)
    return (jnp.dot(hi, sel, preferred_element_type=F32)
            + jnp.dot(lo, sel, preferred_element_type=F32))


def _rwkv_prep_body(pb_ref, first_ref, mu_ref, w0_ref, a0_ref,
                    lora_ref, kk_ref, ka_ref, rk_ref, e_ref, sel_ref,
                    w_o, k_o, v_o, a_o, b_o, wr_o, br_o, kr_o, bonus_o, gate_o, last_ref):
    @pl.when(pl.program_id(1) == 0)
    def _():
        last_ref[...] = jnp.broadcast_to(first_ref[...], last_ref.shape)

    pb_all = pb_ref[...]
    rows = pb_all.shape[0]
    ridx = lax.broadcasted_iota(jnp.int32, pb_all.shape, 0)
    prev = jnp.where(ridx < 1, last_ref[SUBLANES - 1:SUBLANES, :], pltpu.roll(pb_all, 1, 0))
    last_ref[...] = pb_all[rows - SUBLANES:rows, :]
    pm_all = pb_all + (prev - pb_all) * mu_ref[...]
    pm = pm_all[:, 0:3 * D_B]
    r = pm[:, 0:D_B]
    kb = pm[:, D_B:2 * D_B]
    v = pm[:, 2 * D_B:3 * D_B]
    tm_ = pm_all[:, 3 * D_B:3 * D_B + LORA_IN]
    col = lax.broadcasted_iota(jnp.int32, tm_.shape, 1)
    act = jnp.where(col < R_DECAY, jnp.tanh(tm_),
                    jnp.where(col < R_DECAY + R_AAA, tm_, _sigmoid(tm_)))
    act = jnp.where(col < _LORA_USED, act, 0.0).astype(BF16)
    up = jnp.dot(act, lora_ref[...], preferred_element_type=F32)
    w = w0_ref[...] + up[:, 0:D_B]
    decay = jnp.exp(-jnp.exp(-_softplus(-w) - 0.5))
    a = _sigmoid(a0_ref[...] + up[:, D_B:2 * D_B])
    gate_o[...] = up[:, 2 * D_B:3 * D_B]
    e = e_ref[...]
    sel = sel_ref[...]
    kk = kb * kk_ref[...]
    kk = kk / jnp.maximum(jnp.sqrt(_seg_sum(kk * kk, e)), 1e-12)
    k2 = kb * (1.0 + (a - 1.0) * ka_ref[...])
    bvec = kk * a
    w_o[...] = decay
    k_o[...] = k2
    v_o[...] = v
    a_o[...] = -kk
    b_o[...] = bvec
    wr_o[...] = decay * r
    br_o[...] = _seg_sum(bvec * r, sel)
    kr_o[...] = _seg_sum(k2 * r, sel)
    bonus_o[...] = _seg_sum(r * k2 * rk_ref[...], e) * v


def rwkv_prep(pb, shift, p, layer, seq, *, tm=256):
    m = pb.shape[0]
    bsz = m // seq
    tm = min(tm, seq)
    assert seq % tm == 0
    ni = seq // tm
    pad = ((0, 0), (0, PB_PAD - B_PROJ))
    first = jnp.zeros((bsz, 1, PB_PAD), F32) if shift is None else jnp.pad(shift, pad)[:, None]
    mu = jnp.pad(p["rwkv_mu"][layer].reshape(1, B_PROJ), pad)
    lora = jnp.zeros((LORA_IN, 3 * D_B), F32)
    lora = lora.at[0:R_DECAY, 0:D_B].set(p["rwkv_w2"][layer])
    lora = lora.at[R_DECAY:R_DECAY + R_AAA, D_B:2 * D_B].set(p["rwkv_a2"][layer])
    lora = lora.at[R_DECAY + R_AAA:_LORA_USED, 2 * D_B:].set(p["rwkv_g2"][layer]).astype(BF16)
    c = np.arange(D_B)
    e = jnp.asarray((c[:, None] // HD_B) == (c[None, :] // HD_B), BF16)
    sel = jnp.asarray((c[:, None] // HD_B) == np.arange(LANES)[None, :], BF16)
    vec = lambda name: p[name][layer].reshape(1, D_B)
    row = lambda n: pl.BlockSpec((tm, n), lambda b, i: (b * ni + i, 0))
    full = lambda a, b: pl.BlockSpec((a, b), lambda bb, i: (0, 0))
    big = jax.ShapeDtypeStruct((m, D_B), F32)
    small = jax.ShapeDtypeStruct((m, LANES), F32)
    return pl.pallas_call(
        _rwkv_prep_body,
        grid=(bsz, ni),
        in_specs=[row(PB_PAD), pl.BlockSpec((None, 1, PB_PAD), lambda b, i: (b, 0, 0)),
                  full(1, PB_PAD), full(1, D_B), full(1, D_B), full(LORA_IN, 3 * D_B),
                  full(1, D_B), full(1, D_B), full(1, D_B), full(D_B, D_B), full(D_B, LANES)],
        out_specs=[row(D_B)] * 6 + [row(LANES)] * 2 + [row(D_B)] * 2,
        out_shape=[big] * 6 + [small] * 2 + [big] * 2,
        scratch_shapes=[pltpu.VMEM((SUBLANES, PB_PAD), F32)],
        compiler_params=_cparams(2),
        name="rwkv_prep",
    )(pb, first, mu, vec("rwkv_w0"), vec("rwkv_a0"), lora,
      vec("rwkv_k_k"), vec("rwkv_k_a"), p["rwkv_r_k"][layer].reshape(1, D_B), e, sel)


def _rwkv_scan_body(w_ref, k_ref, a_ref, b_ref, wr_ref, v_ref, sc_ref, s0_ref, y_ref, sout_ref,
                    s_scr, *, nch):
    tb = w_ref.shape[0]
    nk = s_scr.shape[0]

    @pl.when(pl.program_id(0) == 0)
    def _():
        s_scr[...] = s0_ref[...]

    def all_parts(x):
        shift = LANES // 2
        while shift >= nch:
            x = x + pltpu.roll(x, shift, 1)
            shift //= 2
        return x

    def step(t, carry):
        sa = None
        yp = None
        for kx in range(nk):
            sk = s_scr[kx]
            pa = sk * a_ref[t, kx:kx + 1, :]
            py = sk * wr_ref[t, kx:kx + 1, :]
            sa = pa if sa is None else sa + pa
            yp = py if yp is None else yp + py
        sa = all_parts(sa)
        yp = all_parts(yp)
        vt = v_ref[t]
        for kx in range(nk):
            s_scr[kx] = (s_scr[kx] * w_ref[t, kx:kx + 1, :] + sa * b_ref[t, kx:kx + 1, :]
                         + vt * k_ref[t, kx:kx + 1, :])
        y_ref[t] = yp + sa * sc_ref[t, 0:1, :] + vt * sc_ref[t, 1:2, :]
        return carry

    lax.fori_loop(0, tb, step, 0)

    @pl.when(pl.program_id(0) == pl.num_programs(0) - 1)
    def _():
        sout_ref[...] = s_scr[...]


def rwkv_scan(w, k, a, b, wr, v, sc, s0, nch, *, tb=16):
    seq, ks, _ = w.shape
    tb = min(tb, seq)
    assert seq % tb == 0
    kblk = pl.BlockSpec((tb, ks, LANES), lambda i: (i, 0, 0))
    vblk = pl.BlockSpec((tb, HD_B, LANES), lambda i: (i, 0, 0))
    sblk = pl.BlockSpec((ks, HD_B, LANES), lambda i: (0, 0, 0))
    return pl.pallas_call(
        functools.partial(_rwkv_scan_body, nch=nch),
        grid=(seq // tb,),
        in_specs=[kblk] * 5 + [vblk, pl.BlockSpec((tb, 2, LANES), lambda i: (i, 0, 0)), sblk],
        out_specs=[vblk, sblk],
        out_shape=[jax.ShapeDtypeStruct((seq, HD_B, LANES), F32),
                   jax.ShapeDtypeStruct((ks, HD_B, LANES), F32)],
        scratch_shapes=[pltpu.VMEM((ks, HD_B, LANES), F32)],
        compiler_params=_cparams(1),
        name="rwkv_scan",
    )(w, k, a, b, wr, v, sc, s0)


def _rwkv_post_body(y_ref, bonus_ref, gate_ref, g_ref, b_ref, e_ref, o_ref):
    e = e_ref[...]
    y = y_ref[...]
    yc = y - _seg_sum(y, e) * (1.0 / HD_B)
    var = _seg_sum(yc * yc, e) * (1.0 / HD_B)
    yn = yc * lax.rsqrt(var + RWKV_GN_EPS) * g_ref[...] + b_ref[...]
    o_ref[...] = ((yn + bonus_ref[...]) * gate_ref[...]).astype(o_ref.dtype)


def rwkv_post(y, bonus, gate, gn_g, gn_b, *, tm=512):
    m = y.shape[0]
    tm = min(tm, m)
    c = np.arange(D_B)
    e = jnp.asarray((c[:, None] // HD_B) == (c[None, :] // HD_B), BF16)
    row = pl.BlockSpec((tm, D_B), lambda i: (i, 0))
    vec = pl.BlockSpec((1, D_B), lambda i: (0, 0))
    return pl.pallas_call(
        _rwkv_post_body,
        grid=(m // tm,),
        in_specs=[row, row, row, vec, vec, pl.BlockSpec((D_B, D_B), lambda i: (0, 0))],
        out_specs=row,
        out_shape=jax.ShapeDtypeStruct((m, D_B), BF16),
        compiler_params=_cparams(1),
        name="rwkv_post",
    )(y, bonus, gate, gn_g.reshape(1, D_B), gn_b.reshape(1, D_B), e)


def rwkv_mixer(pb, shift, s0, p, layer, seq):
    m = pb.shape[0]
    bsz = m // seq
    nch = bsz * H_B
    nsplit = LANES // nch
    ks = HD_B // nsplit
    assert nch * nsplit == LANES
    w, k, v, a, b, wr, br, kr, bonus, gate = rwkv_prep(pb, shift, p, layer, seq)

    def key_major(x):
        x = x.reshape(bsz, seq, H_B, nsplit, ks).transpose(1, 4, 3, 0, 2)
        return x.reshape(seq, ks, LANES)

    def per_chain(x):
        x = jnp.moveaxis(x.reshape((bsz, seq) + x.shape[1:]), 0, -2)
        x = x.reshape(x.shape[:-2] + (nch,))
        return jnp.tile(x, (1,) * (x.ndim - 1) + (nsplit,))

    v_c = per_chain(v.reshape(m, H_B, HD_B).transpose(0, 2, 1))
    sc = per_chain(jnp.stack([br[:, :H_B], kr[:, :H_B]], axis=1))
    if s0 is None:
        s0_c = jnp.zeros((ks, HD_B, LANES), F32)
    else:
        s0_c = s0.reshape(bsz, H_B, HD_B, nsplit, ks).transpose(4, 2, 3, 0, 1).reshape(ks, HD_B, LANES)
    y_c, s_c = rwkv_scan(key_major(w), key_major(k), key_major(a), key_major(b), key_major(wr),
                         v_c, sc, s0_c, nch)
    y = y_c[:, :, :nch].reshape(seq, HD_B, bsz, H_B).transpose(2, 0, 3, 1).reshape(m, D_B)
    s_new = s_c.reshape(ks, HD_B, nsplit, bsz, H_B).transpose(3, 4, 1, 2, 0).reshape(bsz, H_B, HD_B, HD_B)
    o = rwkv_post(y, bonus, gate, p["rwkv_gn_g"][layer], p["rwkv_gn_b"][layer])
    return o, s_new


def even_mixer(xb, seq, s_hgrn, s_rwkv, s_shift, lbs, p, e):
    m = xb.shape[0]
    bsz = m // seq
    pa = matmul(xb, p["w_in_even"], e, n_out=4 * D_A, tn=1024)
    pb = matmul(xb, p["w_in_even"], e, n_out=B_PROJ, col_off=4 * D_A)
    seq_a = -(-seq // HGRN_CHUNK) * HGRN_CHUNK
    pa_p = pa
    if seq_a != seq:
        pa_p = jnp.pad(pa.reshape(bsz, seq, -1), ((0, 0), (0, seq_a - seq), (0, 0))).reshape(bsz * seq_a, -1)
    o_a, h_new = hgrn_mixer(pa_p, lbs[e], p["hgrn_norm_g"][e], s_hgrn, e, seq_a, seq)
    if seq_a != seq:
        o_a = o_a.reshape(bsz, seq_a, D_A)[:, :seq].reshape(m, D_A)
    o_b, r_new = rwkv_mixer(pb, None if s_shift is None else s_shift[e],
                            None if s_rwkv is None else s_rwkv[e], p, e, seq)
    mix = matmul([o_a, o_b], p["w_out_even"], e, tn=1024)
    return mix, h_new, r_new, pb.reshape(bsz, seq, B_PROJ)[:, -1]


SB_TQ = 256
SB_TK = 128
SB_SCALE = HD_C ** -0.5


def _sb_block_sums(sp, lt2):
    hi, lo = _split2(sp)
    return (jnp.dot(hi, lt2, preferred_element_type=F32)
            + jnp.dot(lo, lt2, preferred_element_type=F32))


def _sb_prompt_body(bias_ref, q_ref, k_ref, v_ref, lt2_ref, o_ref, qs_ref, acc_ref, c_ref, *,
                    layer):
    tq, tk, hg = SB_TQ, SB_TK, HEAD_GROUP
    nd = tq // tk
    qi = pl.program_id(2)
    lt2 = lt2_ref[...]
    row = lax.broadcasted_iota(jnp.int32, (tq, tk), 0)
    lane = lax.broadcasted_iota(jnp.int32, (tq, tk), 1)
    qs_ref[...] = (q_ref[...] * SB_SCALE).astype(BF16)
    acc_ref[...] = jnp.zeros_like(acc_ref)
    c_ref[...] = jnp.zeros_like(c_ref)
    heads = range(hg)
    cols = [slice(hd * HD_C, (hd + 1) * HD_C) for hd in heads]
    biases = [bias_ref[layer, pl.program_id(1) * hg + hd] for hd in heads]

    def tile(k0, mask):
        keys = pl.ds(k0, tk)
        z = [lax.dot_general(qs_ref[:, cols[hd]], k_ref[keys, cols[hd]].astype(BF16), _NT,
                             preferred_element_type=F32) + biases[hd] for hd in heads]
        sp = [_softplus(z[hd]) for hd in heads]
        if mask is not None:
            sp = [jnp.where(mask, x, 0.0) for x in sp]
        sums = [_sb_block_sums(sp[hd], lt2) for hd in heads]
        for hd in heads:
            att = jnp.exp(z[hd] - sums[hd][:, 0:tk] - c_ref[hd])
            if mask is not None:
                att = jnp.where(mask, att, 0.0)
            acc_ref[:, cols[hd]] += jnp.dot(att.astype(BF16), v_ref[keys, cols[hd]].astype(BF16),
                                            preferred_element_type=F32)
        for hd in heads:
            c_ref[hd] += sums[hd][:, tk:2 * tk]

    for d in reversed(range(nd)):
        tile(pl.multiple_of(qi * tq + d * tk, tk), (d * tk + lane) < row)

    def full_tile(jj, carry):
        tile(pl.multiple_of((qi * nd - 1 - jj) * tk, tk), None)
        return carry

    lax.fori_loop(0, qi * nd, full_tile, 0)
    o_ref[...] = acc_ref[...].astype(o_ref.dtype)


def _lt2(tk):
    r = np.arange(tk)
    return jnp.asarray(np.concatenate([r[:, None] >= r[None, :], np.ones((tk, tk), bool)], axis=1),
                       BF16)


def sb_prompt(q, k, v, bias, layer, seq):
    m = q.shape[0]
    bsz = m // seq
    tq, hg = SB_TQ, HEAD_GROUP
    assert seq % tq == 0 and H_C % hg == 0
    nq = seq // tq
    ngrp = H_C // hg
    wide = hg * HD_C
    return pl.pallas_call(
        functools.partial(_sb_prompt_body, layer=layer),
        grid=(bsz, ngrp, nq),
        in_specs=[pl.BlockSpec(memory_space=pltpu.SMEM),
                  pl.BlockSpec((tq, wide), lambda b, h, i: (b * nq + i, h)),
                  pl.BlockSpec((seq, wide), lambda b, h, i: (b, h)),
                  pl.BlockSpec((seq, wide), lambda b, h, i: (b, h)),
                  pl.BlockSpec((SB_TK, 2 * SB_TK), lambda b, h, i: (0, 0))],
        out_specs=pl.BlockSpec((tq, wide), lambda b, h, i: (b * nq + i, h)),
        out_shape=jax.ShapeDtypeStruct((m, D_MODEL), BF16),
        scratch_shapes=[pltpu.VMEM((tq, wide), BF16), pltpu.VMEM((tq, wide), F32),
                        pltpu.VMEM((hg, tq, SB_TK), F32)],
        compiler_params=_cparams(3),
        name="sb_prompt",
    )(bias, q, k, v, _lt2(SB_TK))


SB_PAGES_PER_STEP = 4


def _sb_sample_body(pt_ref, q_ref, kn_ref, vn_ref, *rest):
    npg = SB_PAGES_PER_STEP
    kp_refs = rest[0:npg]
    vp_refs = rest[npg:2 * npg]
    bias_ref, lt2_ref, o_ref, qs_ref, acc_ref, c_ref = rest[2 * npg:]
    s = pl.program_id(1)
    nq = q_ref.shape[0]
    lt2 = lt2_ref[...]
    tk = PAGE_SIZE

    def block(k_ref, v_ref, new_keys):
        zs = []
        for h in range(H_C):
            cols = slice(h * HD_C, (h + 1) * HD_C)
            kh = k_ref[pl.ds(h, PAGE_SIZE, stride=H_C), :].astype(BF16)
            zs.append(lax.dot_general(qs_ref[:, cols].astype(BF16), kh, _NT,
                                      preferred_element_type=F32))
        z = jnp.concatenate(zs, axis=0) + bias_ref[...]
        sp = _softplus(z)
        if new_keys:
            s_idx = lax.broadcasted_iota(jnp.int32, z.shape, 1)
            q_idx = lax.broadcasted_iota(jnp.int32, z.shape, 0) % nq
            mask = s_idx < q_idx
            sp = jnp.where(mask, sp, 0.0)
        sums = _sb_block_sums(sp, lt2)
        att = jnp.exp(z - sums[:, 0:tk] - c_ref[...])
        if new_keys:
            att = jnp.where(mask, att, 0.0)
        for h in range(H_C):
            cols = slice(h * HD_C, (h + 1) * HD_C)
            vh = v_ref[pl.ds(h, PAGE_SIZE, stride=H_C), :].astype(BF16)
            acc_ref[:, cols] += jnp.dot(att[h * nq:(h + 1) * nq, :].astype(BF16), vh,
                                        preferred_element_type=F32)
        c_ref[...] += sums[:, tk:2 * tk]

    @pl.when(s == 0)
    def _():
        qs_ref[...] = q_ref[...] * SB_SCALE
        acc_ref[...] = jnp.zeros_like(acc_ref)
        c_ref[...] = jnp.zeros_like(c_ref)
        block(kn_ref, vn_ref, True)

    for j in range(npg):
        block(kp_refs[j], vp_refs[j], False)

    @pl.when(s == pl.num_programs(1) - 1)
    def _():
        o_ref[...] = acc_ref[...].astype(o_ref.dtype)


def sb_sample(q, k, v, cache_k, cache_v, page_table, bias, layer, seq):
    m = q.shape[0]
    bsz = m // seq
    n_pages = page_table.shape[1]
    npg = SB_PAGES_PER_STEP
    assert seq * H_C == LANES and seq <= PAGE_SIZE and n_pages % npg == 0
    rows = PAGE_SIZE * H_C
    ck = cache_k.reshape(cache_k.shape[0], cache_k.shape[1], rows, HD_C)
    cv = cache_v.reshape(cache_v.shape[0], cache_v.shape[1], rows, HD_C)
    pad = ((0, 0), (0, (PAGE_SIZE - seq) * H_C), (0, 0))
    kn = jnp.pad(k.reshape(bsz, seq * H_C, HD_C), pad)
    vn = jnp.pad(v.reshape(bsz, seq * H_C, HD_C), pad)
    bias_rows = jnp.broadcast_to(jnp.repeat(bias[layer], seq)[:, None], (LANES, PAGE_SIZE))
    new = pl.BlockSpec((None, rows, HD_C), lambda b, s, pt: (b, 0, 0))

    def page(j):
        return pl.BlockSpec((None, None, rows, HD_C),
                            lambda b, s, pt: (layer, pt[b, n_pages - 1 - (s * npg + j)], 0, 0))

    pages = [page(j) for j in range(npg)]
    grid_spec = pltpu.PrefetchScalarGridSpec(
        num_scalar_prefetch=1,
        grid=(bsz, n_pages // npg),
        in_specs=[pl.BlockSpec((seq, D_MODEL), lambda b, s, pt: (b, 0)), new, new] + pages + pages
        + [pl.BlockSpec((PAGE_SIZE, PAGE_SIZE), lambda b, s, pt: (0, 0)),
           pl.BlockSpec((PAGE_SIZE, 2 * PAGE_SIZE), lambda b, s, pt: (0, 0))],
        out_specs=pl.BlockSpec((seq, D_MODEL), lambda b, s, pt: (b, 0)),
        scratch_shapes=[pltpu.VMEM((seq, D_MODEL), F32), pltpu.VMEM((seq, D_MODEL), F32),
                        pltpu.VMEM((LANES, PAGE_SIZE), F32)],
    )
    return pl.pallas_call(
        _sb_sample_body,
        grid_spec=grid_spec,
        out_shape=jax.ShapeDtypeStruct((m, D_MODEL), BF16),
        compiler_params=_cparams(2),
        name="sb_sample",
    )(page_table, q, kn, vn, *([ck] * npg), *([cv] * npg), bias_rows, _lt2(PAGE_SIZE))


def sb_mixer(xb, seq, cache_k, cache_v, page_table, p, o):
    m = xb.shape[0]
    bsz = m // seq
    q, k, v = (matmul(xb, p["w_qkv_odd"], o, n_out=D_MODEL, col_off=c * D_MODEL, tn=1024)
               for c in range(3))
    if cache_k is None:
        att = sb_prompt(q, k, v, p["sb_bias"], o, seq)
    else:
        att = sb_sample(q, k, v, cache_k, cache_v, page_table, p["sb_bias"], o, seq)
    mix = matmul(att, p["w_out_odd"], o, tn=1024)
    return mix, k.reshape(bsz, seq, H_C, HD_C), v.reshape(bsz, seq, H_C, HD_C)


def conv_ffn(xb, seq, buf, p, layer):
    if buf is None:
        a, buf_new = ffn_up_fresh(xb, p["ffn_up"], p["ffn_conv_w"], p["ffn_conv_b"], layer, seq)
    else:
        h = matmul(xb, p["ffn_up"], layer, tn=1024)
        a, buf_new = conv_geglu(h, buf, p["ffn_conv_w"], p["ffn_conv_b"], layer, seq, tn=D_FF)
    return matmul(a, p["ffn_down"], layer, tm=512), buf_new


def _hgrn_lower_bounds(lb_param):
    lbs = jnp.cumsum(jax.nn.softmax(lb_param.astype(F32), axis=0), axis=0)
    return lbs - lbs[0:1]


def kernel(x_prompt, x_sample, cache_k, cache_v, page_table, state_hgrn, state_rwkv, state_rwkv_shift, state_ffn_conv, w_in_even, hgrn_lb_param, hgrn_norm_g, rwkv_mu, rwkv_w0, rwkv_w2, rwkv_a0, rwkv_a2, rwkv_g2, rwkv_k_k, rwkv_k_a, rwkv_r_k, rwkv_gn_g, rwkv_gn_b, w_out_even, w_qkv_odd, w_out_odd, sb_bias, ln1_g, ln1_b, ln2_g, ln2_b, ffn_up, ffn_conv_w, ffn_conv_b, ffn_down):
    p = dict(w_in_even=w_in_even, hgrn_norm_g=hgrn_norm_g, rwkv_mu=rwkv_mu, rwkv_w0=rwkv_w0,
             rwkv_w2=rwkv_w2, rwkv_a0=rwkv_a0, rwkv_a2=rwkv_a2, rwkv_g2=rwkv_g2, rwkv_k_k=rwkv_k_k,
             rwkv_k_a=rwkv_k_a, rwkv_r_k=rwkv_r_k, rwkv_gn_g=rwkv_gn_g, rwkv_gn_b=rwkv_gn_b,
             w_out_even=w_out_even, w_qkv_odd=w_qkv_odd, w_out_odd=w_out_odd, sb_bias=sb_bias,
             ffn_up=ffn_up, ffn_conv_w=ffn_conv_w, ffn_conv_b=ffn_conv_b, ffn_down=ffn_down)
    lbs = _hgrn_lower_bounds(hgrn_lb_param)
    bp, lp, _ = x_prompt.shape
    bs, ls, _ = x_sample.shape
    groups = [
        dict(x=x_prompt.reshape(bp * lp, D_MODEL), seq=lp, sample=False),
        dict(x=x_sample.reshape(bs * ls, D_MODEL), seq=ls, sample=True),
    ]
    for g in groups:
        g["xb"] = g["x"].astype(BF16)
        g["k"], g["v"], g["hgrn"], g["rwkv"], g["shift"], g["conv"] = [], [], [], [], [], []
    for layer in range(DEPTH):
        for g in groups:
            seq, sample = g["seq"], g["sample"]
            if layer % 2 == 0:
                e = layer // 2
                mix, h_new, r_new, sh_new = even_mixer(
                    g["xb"], seq, state_hgrn if sample else None, state_rwkv if sample else None,
                    state_rwkv_shift if sample else None, lbs, p, e)
                g["hgrn"].append(h_new)
                g["rwkv"].append(r_new)
                g["shift"].append(sh_new)
            else:
                o = layer // 2
                mix, k_new, v_new = sb_mixer(g["xb"], seq, cache_k if sample else None,
                                             cache_v if sample else None, page_table, p, o)
                g["k"].append(k_new)
                g["v"].append(v_new)
            g["x"], g["xb"] = add_layer_norm(g["x"], mix, ln1_g, ln1_b, layer)
            f, c_new = conv_ffn(g["xb"], seq, state_ffn_conv if sample else None, p, layer)
            g["conv"].append(c_new)
            g["x"], g["xb"] = add_layer_norm(g["x"], f, ln2_g, ln2_b, layer)
    gp, gs = groups
    st = jnp.stack
    return (gp["x"].reshape(bp, lp, D_MODEL), gs["x"].reshape(bs, ls, D_MODEL),
            st(gp["k"]), st(gp["v"]), st(gs["k"]), st(gs["v"]),
            st(gp["hgrn"]), st(gs["hgrn"]), st(gp["rwkv"]), st(gs["rwkv"]),
            st(gp["shift"]), st(gs["shift"]), st(gp["conv"]), st(gs["conv"]))
```

```python
import functools

import jax
import jax.numpy as jnp
import numpy as np
from jax import lax
from jax.experimental import pallas as pl
from jax.experimental.pallas import tpu as pltpu

F32 = jnp.float32
BF16 = jnp.bfloat16

D_MODEL = 2048
DEPTH = 4
PAGE_SIZE = 128
D_A = D_MODEL // 2
DK_A = 128
H_A = D_A // DK_A
DV_A = D_A // H_A
D_B = D_MODEL - D_A
HD_B = 64
H_B = D_B // HD_B
R_DECAY = 64
R_AAA = 64
R_GATE = 160
B_PROJ = 3 * D_B + R_DECAY + R_AAA + R_GATE
RWKV_GN_EPS = 64e-5
HD_C = 128
H_C = D_MODEL // HD_C
D_FF = 5632
CONV_W = 3
ALPHA = (2 * DEPTH) ** 0.25
LN_EPS = 1e-5
RMS_EPS = 1e-6

LANES = 128
SUBLANES = 8
HGRN_CHUNK = 16
HEAD_GROUP = 4
VMEM_LIMIT = 56 * 1024 * 1024


def _cparams(n_axes):
    return pltpu.CompilerParams(dimension_semantics=("arbitrary",) * n_axes,
                                vmem_limit_bytes=VMEM_LIMIT)


def _split2(x):
    hi = x.astype(BF16)
    lo = (x - hi.astype(F32)).astype(BF16)
    return hi, lo


def _split3(x):
    hi = x.astype(BF16)
    r = x - hi.astype(F32)
    mid = r.astype(BF16)
    lo = (r - mid.astype(F32)).astype(BF16)
    return hi, mid, lo


def _softplus(x):
    return jnp.maximum(x, 0.0) + jnp.log1p(jnp.exp(-jnp.abs(x)))


def _sigmoid(x):
    return 1.0 / (1.0 + jnp.exp(-x))


def _mm_body(*refs):
    *x_refs, w_ref, o_ref, wb_ref = refs

    @pl.when(pl.program_id(1) == 0)
    def _():
        wb_ref[...] = w_ref[...].astype(BF16)

    acc = None
    k0 = 0
    for x_ref in x_refs:
        k1 = k0 + x_ref.shape[1]
        part = jnp.dot(x_ref[...].astype(BF16), wb_ref[k0:k1, :], preferred_element_type=F32)
        acc = part if acc is None else acc + part
        k0 = k1
    o_ref[...] = acc.astype(o_ref.dtype)


def matmul(xs, w, layer, *, n_out=None, col_off=0, tm=1024, tn=512, out_dtype=F32):
    xs = xs if isinstance(xs, (list, tuple)) else [xs]
    m = xs[0].shape[0]
    k = sum(x.shape[1] for x in xs)
    n = n_out or w.shape[2]
    tm = min(tm, m)
    tn = min(tn, n)
    assert m % tm == 0 and col_off % tn == 0 and k == w.shape[1]
    joff = col_off // tn
    return pl.pallas_call(
        _mm_body,
        grid=(pl.cdiv(n, tn), m // tm),
        in_specs=[pl.BlockSpec((tm, x.shape[1]), lambda j, i: (i, 0)) for x in xs]
        + [pl.BlockSpec((None, k, tn), lambda j, i: (layer, 0, j + joff))],
        out_specs=pl.BlockSpec((tm, tn), lambda j, i: (i, j)),
        out_shape=jax.ShapeDtypeStruct((m, n), out_dtype),
        scratch_shapes=[pltpu.VMEM((k, tn), BF16)],
        compiler_params=_cparams(2),
        name="matmul",
    )(*xs, w)


def _ln_body(x_ref, m_ref, g_ref, b_ref, o_ref, ob_ref):
    v = ALPHA * x_ref[...] + m_ref[...]
    mu = jnp.mean(v, axis=-1, keepdims=True)
    vc = v - mu
    var = jnp.mean(vc * vc, axis=-1, keepdims=True)
    y = vc * lax.rsqrt(var + LN_EPS) * g_ref[...] + b_ref[...]
    o_ref[...] = y
    ob_ref[...] = y.astype(BF16)


def add_layer_norm(x, mix, g, b, layer, *, tm=512):
    m, d = x.shape
    tm = min(tm, m)
    g3 = g.reshape(g.shape[0], 1, d)
    b3 = b.reshape(b.shape[0], 1, d)
    row = pl.BlockSpec((tm, d), lambda i: (i, 0))
    par = pl.BlockSpec((None, 1, d), lambda i: (layer, 0, 0))
    return pl.pallas_call(
        _ln_body,
        grid=(m // tm,),
        in_specs=[row, row, par, par],
        out_specs=[row, row],
        out_shape=[jax.ShapeDtypeStruct((m, d), F32), jax.ShapeDtypeStruct((m, d), BF16)],
        compiler_params=_cparams(1),
        name="add_layer_norm",
    )(x, mix, g3, b3)


def _causal_conv(h, p2, p1, cw, cb, ridx):
    hm1 = jnp.where(ridx < 1, p1, pltpu.roll(h, 1, 0))
    hm2 = jnp.where(ridx < 1, p2, jnp.where(ridx < 2, p1, pltpu.roll(h, 2, 0)))
    return cb + cw[0:1, :] * hm2 + cw[1:2, :] * hm1 + cw[2:3, :] * h


def _geglu(u, g):
    return 0.5 * g * (1.0 + lax.erf(g * (2.0 ** -0.5))) * u


def _conv_geglu_body(hu_ref, hg_ref, bu_ref, bg_ref, cwu_ref, cwg_ref, cbu_ref, cbg_ref,
                     a_ref, su_ref, sg_ref):
    rows = hu_ref.shape[0]
    ridx = lax.broadcasted_iota(jnp.int32, hu_ref.shape, 0)

    def conv(h_ref, buf_ref, cw_ref, cb_ref, s_ref):
        h = h_ref[...]
        buf = buf_ref[...]
        s_ref[...] = h[rows - 2:rows, :]
        return _causal_conv(h, buf[0:1, :], buf[1:2, :], cw_ref[...], cb_ref[...], ridx)

    u = conv(hu_ref, bu_ref, cwu_ref, cbu_ref, su_ref)
    g = conv(hg_ref, bg_ref, cwg_ref, cbg_ref, sg_ref)
    a_ref[...] = _geglu(u, g).astype(a_ref.dtype)


def conv_geglu(h, buf, conv_w, conv_b, layer, seq, *, tn=512):
    m = h.shape[0]
    bsz = m // seq
    nj = D_FF // tn
    cb3 = conv_b.reshape(conv_b.shape[0], 1, 2 * D_FF)
    hu = pl.BlockSpec((seq, tn), lambda b, j: (b, j))
    hg = pl.BlockSpec((seq, tn), lambda b, j: (b, j + nj))
    bu = pl.BlockSpec((None, None, CONV_W - 1, tn), lambda b, j: (layer, b, 0, j))
    bg = pl.BlockSpec((None, None, CONV_W - 1, tn), lambda b, j: (layer, b, 0, j + nj))
    cwu = pl.BlockSpec((None, CONV_W, tn), lambda b, j: (layer, 0, j))
    cwg = pl.BlockSpec((None, CONV_W, tn), lambda b, j: (layer, 0, j + nj))
    cbu = pl.BlockSpec((None, 1, tn), lambda b, j: (layer, 0, j))
    cbg = pl.BlockSpec((None, 1, tn), lambda b, j: (layer, 0, j + nj))
    st = pl.BlockSpec((None, CONV_W - 1, tn), lambda b, j: (b, 0, j))
    a, su, sg = pl.pallas_call(
        _conv_geglu_body,
        grid=(bsz, nj),
        in_specs=[hu, hg, bu, bg, cwu, cwg, cbu, cbg],
        out_specs=[pl.BlockSpec((seq, tn), lambda b, j: (b, j)), st, st],
        out_shape=[jax.ShapeDtypeStruct((m, D_FF), BF16),
                   jax.ShapeDtypeStruct((bsz, CONV_W - 1, D_FF), F32),
                   jax.ShapeDtypeStruct((bsz, CONV_W - 1, D_FF), F32)],
        compiler_params=_cparams(2),
        name="conv_geglu",
    )(h, h, buf, buf, conv_w, conv_w, cb3, cb3)
    return a, jnp.concatenate([su, sg], axis=-1)


def _ffn_up_fresh_body(x_ref, wu_ref, wg_ref, cwu_ref, cwg_ref, cbu_ref, cbg_ref,
                       a_ref, su_ref, sg_ref, wub_ref, wgb_ref, pu_ref, pg_ref):
    b = pl.program_id(1)
    i = pl.program_id(2)

    @pl.when((b == 0) & (i == 0))
    def _():
        wub_ref[...] = wu_ref[...].astype(BF16)
        wgb_ref[...] = wg_ref[...].astype(BF16)

    @pl.when(i == 0)
    def _():
        pu_ref[...] = jnp.zeros_like(pu_ref)
        pg_ref[...] = jnp.zeros_like(pg_ref)

    x = x_ref[...]
    rows = x.shape[0]
    ridx = lax.broadcasted_iota(jnp.int32, (rows, wub_ref.shape[1]), 0)

    def half(wb_ref, prev_ref, cw_ref, cb_ref, s_ref):
        h = jnp.dot(x, wb_ref[...], preferred_element_type=F32)
        prev = prev_ref[...]
        out = _causal_conv(h, prev[SUBLANES - 2:SUBLANES - 1, :], prev[SUBLANES - 1:SUBLANES, :],
                           cw_ref[...], cb_ref[...], ridx)
        prev_ref[...] = h[rows - SUBLANES:rows, :]
        s_ref[...] = h[rows - 2:rows, :]
        return out

    u = half(wub_ref, pu_ref, cwu_ref, cbu_ref, su_ref)
    g = half(wgb_ref, pg_ref, cwg_ref, cbg_ref, sg_ref)
    a_ref[...] = _geglu(u, g).astype(a_ref.dtype)


def ffn_up_fresh(xb, w_up, conv_w, conv_b, layer, seq, *, tm=1024, tn=512):
    m, k = xb.shape
    bsz = m // seq
    tm = min(tm, seq)
    assert seq % tm == 0 and D_FF % tn == 0
    ni = seq // tm
    nj = D_FF // tn
    cb3 = conv_b.reshape(conv_b.shape[0], 1, 2 * D_FF)
    st = pl.BlockSpec((None, CONV_W - 1, tn), lambda j, b, i: (b, 0, j))
    a, su, sg = pl.pallas_call(
        _ffn_up_fresh_body,
        grid=(nj, bsz, ni),
        in_specs=[pl.BlockSpec((tm, k), lambda j, b, i: (b * ni + i, 0)),
                  pl.BlockSpec((None, k, tn), lambda j, b, i: (layer, 0, j)),
                  pl.BlockSpec((None, k, tn), lambda j, b, i: (layer, 0, j + nj)),
                  pl.BlockSpec((None, CONV_W, tn), lambda j, b, i: (layer, 0, j)),
                  pl.BlockSpec((None, CONV_W, tn), lambda j, b, i: (layer, 0, j + nj)),
                  pl.BlockSpec((None, 1, tn), lambda j, b, i: (layer, 0, j)),
                  pl.BlockSpec((None, 1, tn), lambda j, b, i: (layer, 0, j + nj))],
        out_specs=[pl.BlockSpec((tm, tn), lambda j, b, i: (b * ni + i, j)), st, st],
        out_shape=[jax.ShapeDtypeStruct((m, D_FF), BF16),
                   jax.ShapeDtypeStruct((bsz, CONV_W - 1, D_FF), F32),
                   jax.ShapeDtypeStruct((bsz, CONV_W - 1, D_FF), F32)],
        scratch_shapes=[pltpu.VMEM((k, tn), BF16), pltpu.VMEM((k, tn), BF16),
                        pltpu.VMEM((SUBLANES, tn), F32), pltpu.VMEM((SUBLANES, tn), F32)],
        compiler_params=_cparams(3),
        name="ffn_up_fresh",
    )(xb, w_up, w_up, conv_w, conv_w, cb3, cb3)
    return a, jnp.concatenate([su, sg], axis=-1)


_NT = (((1,), (1,)), ((), ()))
_TN = (((0,), (0,)), ((), ()))


def _hgrn_body(q_ref, f_ref, i_ref, g_ref, lb_ref, ng_ref, s0_ref, tb_ref,
               o_ref, s_ref, st_ref, *, valid):
    seq = q_ref.shape[0]
    slab = tb_ref.shape[1]
    c = HGRN_CHUNK
    heads = range(HEAD_GROUP)
    cols = [slice(hd * DK_A, (hd + 1) * DK_A) for hd in heads]
    tb = tb_ref[...]
    causal = tb[0:slab, :] > 0
    for hd in heads:
        st_ref[hd] = s0_ref[hd].T

    def slab_step(si, carry):
        r0 = pl.multiple_of(si * slab, slab)
        rows = pl.ds(r0, slab)
        q, k, parts = [], [], []
        for hd in heads:
            lb = lb_ref[:, cols[hd]]
            log_lb = jnp.log(lb)
            qa = q_ref[rows, cols[hd]]
            fa = f_ref[rows, cols[hd]]
            x2 = jnp.log1p(-lb) + (jnp.minimum(fa, 0.0) - jnp.log1p(jnp.exp(-jnp.abs(fa))))
            mx = jnp.maximum(log_lb, x2)
            log_f = mx + jnp.log(jnp.exp(log_lb - mx) + jnp.exp(x2 - mx))
            kh = (1.0 - lb) * _sigmoid(-fa)
            if valid < seq:
                live = (r0 + lax.broadcasted_iota(jnp.int32, fa.shape, 0)) < valid
                log_f = jnp.where(live, log_f, 0.0)
                kh = jnp.where(live, kh, 0.0)
            q.append(qa * _sigmoid(qa))
            k.append(kh)
            parts.append(_split3(log_f))
        gg = [sum(jnp.dot(tb, p, preferred_element_type=F32) for p in parts[hd]) for hd in heads]
        qg, kinv, kg, dec, vb = [], [], [], [], []
        for hd in heads:
            g = gg[hd][0:slab, :]
            gl = gg[hd][slab:2 * slab, :]
            qg.append((q[hd] * jnp.exp(g)).astype(BF16))
            kinv.append((k[hd] * jnp.exp(-g)).astype(BF16))
            kg.append((k[hd] * jnp.exp(gl - g)).astype(BF16))
            dec.append(jnp.exp(gl))
            vb.append(i_ref[rows, cols[hd]].astype(BF16))
        att = [lax.dot_general(qg[hd], kinv[hd], _NT, preferred_element_type=F32) for hd in heads]
        o_in = [jnp.dot(jnp.where(causal, att[hd], 0.0).astype(BF16), vb[hd],
                        preferred_element_type=F32) for hd in heads]
        st = [st_ref[hd] for hd in heads]
        outs = [[] for _ in heads]
        for ci in range(slab // c):
            rs = slice(ci * c, (ci + 1) * c)
            for hd in heads:
                outs[hd].append(o_in[hd][rs] + lax.dot_general(
                    qg[hd][rs], st[hd].astype(BF16), _NT, preferred_element_type=F32))
                u = lax.dot_general(vb[hd][rs], kg[hd][rs], _TN, preferred_element_type=F32)
                st[hd] = st[hd] * dec[hd][ci * c:ci * c + 1, :] + u
        for hd in heads:
            st_ref[hd] = st[hd]
            o = jnp.concatenate(outs[hd], axis=0) if len(outs[hd]) > 1 else outs[hd][0]
            o = o * lax.rsqrt(jnp.mean(o * o, axis=-1, keepdims=True) + RMS_EPS) * ng_ref[:, cols[hd]]
            ga = g_ref[rows, cols[hd]]
            o_ref[rows, cols[hd]] = (o * (ga * _sigmoid(ga))).astype(o_ref.dtype)
        return carry

    lax.fori_loop(0, seq // slab, slab_step, 0)
    for hd in heads:
        s_ref[hd] = st_ref[hd].T


def hgrn_mixer(pa, lb, norm_g, s0, layer, seq, valid):
    m = pa.shape[0]
    bsz = m // seq
    slab = min(LANES, seq)
    hg = HEAD_GROUP
    assert seq % slab == 0 and slab % HGRN_CHUNK == 0 and H_A % hg == 0
    if s0 is None:
        s0 = jnp.zeros((1, bsz, H_A, DK_A, DV_A), F32)
        slayer = 0
    else:
        slayer = layer
    r = np.arange(slab)
    same = (r[:, None] // HGRN_CHUNK) == (r[None, :] // HGRN_CHUNK)
    tb = jnp.asarray(np.concatenate([same & (r[None, :] <= r[:, None]), same], axis=0), BF16)
    ngrp = H_A // hg
    col = lambda off: pl.BlockSpec((seq, hg * DK_A), lambda b, h: (b, h + off * ngrp))
    vec = pl.BlockSpec((1, hg * DK_A), lambda b, h: (0, h))
    const = pl.BlockSpec((2 * slab, slab), lambda b, h: (0, 0))
    o, s = pl.pallas_call(
        functools.partial(_hgrn_body, valid=valid),
        grid=(bsz, ngrp),
        in_specs=[col(0), col(1), col(2), col(3), vec, vec,
                  pl.BlockSpec((None, None, hg, DK_A, DV_A), lambda b, h: (slayer, b, h, 0, 0)),
                  const],
        out_specs=[pl.BlockSpec((seq, hg * DV_A), lambda b, h: (b, h)),
                   pl.BlockSpec((None, hg, DK_A, DV_A), lambda b, h: (b, h, 0, 0))],
        out_shape=[jax.ShapeDtypeStruct((m, D_A), BF16),
                   jax.ShapeDtypeStruct((bsz, H_A, DK_A, DV_A), F32)],
        scratch_shapes=[pltpu.VMEM((hg, DV_A, DK_A), F32)],
        compiler_params=_cparams(2),
        name="hgrn_mixer",
    )(pa, pa, pa, pa, lb.reshape(1, D_A), norm_g.reshape(1, D_A), s0, tb)
    return o, s


LORA_IN = 3 * LANES
PB_PAD = 3 * D_B + LORA_IN
_LORA_USED = R_DECAY + R_AAA + R_GATE


def _head_selectors():
    c = np.arange(D_B)
    sel = (c[:, None] // HD_B) == np.arange(LANES)[None, :]
    return jnp.asarray(sel, BF16), jnp.asarray(sel.T, BF16)


def _seg_sum(x, sel):
    hi, lo = _split2(x)
    return (jnp.dot(hi, sel, preferred_element_type=F32)
            + jnp.dot(lo, sel, preferred_element_type=F32))


def _rwkv_prep_body(pb_ref, first_ref, mu_ref, w0_ref, a0_ref,
                    lora_ref, kk_ref, ka_ref, rk_ref, ex_ref, sel_ref,
                    w_o, k_o, v_o, a_o, b_o, wr_o, br_o, kr_o, bonus_o, gate_o, last_ref):
    @pl.when(pl.program_id(1) == 0)
    def _():
        last_ref[...] = jnp.broadcast_to(first_ref[...], last_ref.shape)

    pb_all = pb_ref[...]
    rows = pb_all.shape[0]
    ridx = lax.broadcasted_iota(jnp.int32, pb_all.shape, 0)
    prev = jnp.where(ridx < 1, last_ref[SUBLANES - 1:SUBLANES, :], pltpu.roll(pb_all, 1, 0))
    last_ref[...] = pb_all[rows - SUBLANES:rows, :]
    pm_all = pb_all + (prev - pb_all) * mu_ref[...]
    pm = pm_all[:, 0:3 * D_B]
    r = pm[:, 0:D_B]
    kb = pm[:, D_B:2 * D_B]
    v = pm[:, 2 * D_B:3 * D_B]
    tm_ = pm_all[:, 3 * D_B:3 * D_B + LORA_IN]
    col = lax.broadcasted_iota(jnp.int32, tm_.shape, 1)
    act = jnp.where(col < R_DECAY, jnp.tanh(tm_),
                    jnp.where(col < R_DECAY + R_AAA, tm_, _sigmoid(tm_)))
    act = jnp.where(col < _LORA_USED, act, 0.0).astype(BF16)
    up = jnp.dot(act, lora_ref[...], preferred_element_type=F32)
    w = w0_ref[...] + up[:, 0:D_B]
    decay = jnp.exp(-jnp.exp(-_softplus(-w) - 0.5))
    a = _sigmoid(a0_ref[...] + up[:, D_B:2 * D_B])
    gate_o[...] = up[:, 2 * D_B:3 * D_B]
    ex = ex_ref[...]
    sel = sel_ref[...]
    kk = kb * kk_ref[...]
    kk = kk / jnp.maximum(jnp.sqrt(_seg_sum(_seg_sum(kk * kk, sel), ex)), 1e-12)
    k2 = kb * (1.0 + (a - 1.0) * ka_ref[...])
    bvec = kk * a
    w_o[...] = decay
    k_o[...] = k2
    v_o[...] = v
    a_o[...] = -kk
    b_o[...] = bvec
    wr_o[...] = decay * r
    br_o[...] = _seg_sum(bvec * r, sel)
    kr_o[...] = _seg_sum(k2 * r, sel)
    bonus_o[...] = _seg_sum(_seg_sum(r * k2 * rk_ref[...], sel), ex) * v


def rwkv_prep(pb, shift, p, layer, seq, *, tm=256):
    m = pb.shape[0]
    bsz = m // seq
    tm = min(tm, seq)
    assert seq % tm == 0
    ni = seq // tm
    pad = ((0, 0), (0, PB_PAD - B_PROJ))
    first = jnp.zeros((bsz, 1, PB_PAD), F32) if shift is None else jnp.pad(shift, pad)[:, None]
    mu = jnp.pad(p["rwkv_mu"][layer].reshape(1, B_PROJ), pad)
    lora = jnp.zeros((LORA_IN, 3 * D_B), F32)
    lora = lora.at[0:R_DECAY, 0:D_B].set(p["rwkv_w2"][layer])
    lora = lora.at[R_DECAY:R_DECAY + R_AAA, D_B:2 * D_B].set(p["rwkv_a2"][layer])
    lora = lora.at[R_DECAY + R_AAA:_LORA_USED, 2 * D_B:].set(p["rwkv_g2"][layer]).astype(BF16)
    sel, ex = _head_selectors()
    vec = lambda name: p[name][layer].reshape(1, D_B)
    row = lambda n: pl.BlockSpec((tm, n), lambda b, i: (b * ni + i, 0))
    full = lambda a, b: pl.BlockSpec((a, b), lambda bb, i: (0, 0))
    big = jax.ShapeDtypeStruct((m, D_B), F32)
    small = jax.ShapeDtypeStruct((m, LANES), F32)
    return pl.pallas_call(
        _rwkv_prep_body,
        grid=(bsz, ni),
        in_specs=[row(PB_PAD), pl.BlockSpec((None, 1, PB_PAD), lambda b, i: (b, 0, 0)),
                  full(1, PB_PAD), full(1, D_B), full(1, D_B), full(LORA_IN, 3 * D_B),
                  full(1, D_B), full(1, D_B), full(1, D_B), full(LANES, D_B), full(D_B, LANES)],
        out_specs=[row(D_B)] * 6 + [row(LANES)] * 2 + [row(D_B)] * 2,
        out_shape=[big] * 6 + [small] * 2 + [big] * 2,
        scratch_shapes=[pltpu.VMEM((SUBLANES, PB_PAD), F32)],
        compiler_params=_cparams(2),
        name="rwkv_prep",
    )(pb, first, mu, vec("rwkv_w0"), vec("rwkv_a0"), lora,
      vec("rwkv_k_k"), vec("rwkv_k_a"), p["rwkv_r_k"][layer].reshape(1, D_B), ex, sel)


def _rwkv_scan_body(w_ref, k_ref, a_ref, b_ref, wr_ref, v_ref, sc_ref, s0_ref, y_ref, sout_ref,
                    s_scr, *, nch):
    tb = w_ref.shape[0]
    nk = s_scr.shape[0]

    @pl.when(pl.program_id(0) == 0)
    def _():
        s_scr[...] = s0_ref[...]

    def all_parts(x):
        shift = LANES // 2
        while shift >= nch:
            x = x + pltpu.roll(x, shift, 1)
            shift //= 2
        return x

    def step(t, carry):
        sa = None
        yp = None
        for kx in range(nk):
            sk = s_scr[kx]
            pa = sk * a_ref[t, kx:kx + 1, :]
            py = sk * wr_ref[t, kx:kx + 1, :]
            sa = pa if sa is None else sa + pa
            yp = py if yp is None else yp + py
        sa = all_parts(sa)
        yp = all_parts(yp)
        vt = v_ref[t]
        for kx in range(nk):
            s_scr[kx] = (s_scr[kx] * w_ref[t, kx:kx + 1, :] + sa * b_ref[t, kx:kx + 1, :]
                         + vt * k_ref[t, kx:kx + 1, :])
        y_ref[t] = yp + sa * sc_ref[t, 0:1, :] + vt * sc_ref[t, 1:2, :]
        return carry

    lax.fori_loop(0, tb, step, 0)

    @pl.when(pl.program_id(0) == pl.num_programs(0) - 1)
    def _():
        sout_ref[...] = s_scr[...]


def rwkv_scan(w, k, a, b, wr, v, sc, s0, nch, *, tb=64):
    seq, ks, _ = w.shape
    tb = min(tb, seq)
    assert seq % tb == 0
    kblk = pl.BlockSpec((tb, ks, LANES), lambda i: (i, 0, 0))
    vblk = pl.BlockSpec((tb, HD_B, LANES), lambda i: (i, 0, 0))
    sblk = pl.BlockSpec((ks, HD_B, LANES), lambda i: (0, 0, 0))
    return pl.pallas_call(
        functools.partial(_rwkv_scan_body, nch=nch),
        grid=(seq // tb,),
        in_specs=[kblk] * 5 + [vblk, pl.BlockSpec((tb, 2, LANES), lambda i: (i, 0, 0)), sblk],
        out_specs=[vblk, sblk],
        out_shape=[jax.ShapeDtypeStruct((seq, HD_B, LANES), F32),
                   jax.ShapeDtypeStruct((ks, HD_B, LANES), F32)],
        scratch_shapes=[pltpu.VMEM((ks, HD_B, LANES), F32)],
        compiler_params=_cparams(1),
        name="rwkv_scan",
    )(w, k, a, b, wr, v, sc, s0)


def _rwkv_post_body(y_ref, bonus_ref, gate_ref, g_ref, b_ref, sel_ref, ex_ref, o_ref):
    sel = sel_ref[...]
    ex = ex_ref[...]
    y = y_ref[...]
    yc = y - _seg_sum(_seg_sum(y, sel), ex) * (1.0 / HD_B)
    var = _seg_sum(_seg_sum(yc * yc, sel), ex) * (1.0 / HD_B)
    yn = yc * lax.rsqrt(var + RWKV_GN_EPS) * g_ref[...] + b_ref[...]
    o_ref[...] = ((yn + bonus_ref[...]) * gate_ref[...]).astype(o_ref.dtype)


def rwkv_post(y, bonus, gate, gn_g, gn_b, *, tm=512):
    m = y.shape[0]
    tm = min(tm, m)
    sel, ex = _head_selectors()
    row = pl.BlockSpec((tm, D_B), lambda i: (i, 0))
    vec = pl.BlockSpec((1, D_B), lambda i: (0, 0))
    return pl.pallas_call(
        _rwkv_post_body,
        grid=(m // tm,),
        in_specs=[row, row, row, vec, vec, pl.BlockSpec((D_B, LANES), lambda i: (0, 0)),
                  pl.BlockSpec((LANES, D_B), lambda i: (0, 0))],
        out_specs=row,
        out_shape=jax.ShapeDtypeStruct((m, D_B), BF16),
        compiler_params=_cparams(1),
        name="rwkv_post",
    )(y, bonus, gate, gn_g.reshape(1, D_B), gn_b.reshape(1, D_B), sel, ex)


def rwkv_mixer(pb, shift, s0, p, layer, seq):
    m = pb.shape[0]
    bsz = m // seq
    nch = bsz * H_B
    nsplit = LANES // nch
    ks = HD_B // nsplit
    assert nch * nsplit == LANES
    w, k, v, a, b, wr, br, kr, bonus, gate = rwkv_prep(pb, shift, p, layer, seq)

    def key_major(x):
        x = x.reshape(bsz, seq, H_B, nsplit, ks).transpose(1, 4, 3, 0, 2)
        return x.reshape(seq, ks, LANES)

    def per_chain(x):
        x = jnp.moveaxis(x.reshape((bsz, seq) + x.shape[1:]), 0, -2)
        x = x.reshape(x.shape[:-2] + (nch,))
        return jnp.tile(x, (1,) * (x.ndim - 1) + (nsplit,))

    v_c = per_chain(v.reshape(m, H_B, HD_B).transpose(0, 2, 1))
    sc = per_chain(jnp.stack([br[:, :H_B], kr[:, :H_B]], axis=1))
    if s0 is None:
        s0_c = jnp.zeros((ks, HD_B, LANES), F32)
    else:
        s0_c = s0.reshape(bsz, H_B, HD_B, nsplit, ks).transpose(4, 2, 3, 0, 1).reshape(ks, HD_B, LANES)
    y_c, s_c = rwkv_scan(key_major(w), key_major(k), key_major(a), key_major(b), key_major(wr),
                         v_c, sc, s0_c, nch)
    y = y_c[:, :, :nch].reshape(seq, HD_B, bsz, H_B).transpose(2, 0, 3, 1).reshape(m, D_B)
    s_new = s_c.reshape(ks, HD_B, nsplit, bsz, H_B).transpose(3, 4, 1, 2, 0).reshape(bsz, H_B, HD_B, HD_B)
    o = rwkv_post(y, bonus, gate, p["rwkv_gn_g"][layer], p["rwkv_gn_b"][layer])
    return o, s_new


def even_mixer(xb, seq, s_hgrn, s_rwkv, s_shift, lbs, p, e):
    m = xb.shape[0]
    bsz = m // seq
    pa = matmul(xb, p["w_in_even"], e, n_out=4 * D_A, tn=1024)
    pb = matmul(xb, p["w_in_even"], e, n_out=B_PROJ, col_off=4 * D_A)
    seq_a = -(-seq // HGRN_CHUNK) * HGRN_CHUNK
    pa_p = pa
    if seq_a != seq:
        pa_p = jnp.pad(pa.reshape(bsz, seq, -1), ((0, 0), (0, seq_a - seq), (0, 0))).reshape(bsz * seq_a, -1)
    o_a, h_new = hgrn_mixer(pa_p, lbs[e], p["hgrn_norm_g"][e], s_hgrn, e, seq_a, seq)
    if seq_a != seq:
        o_a = o_a.reshape(bsz, seq_a, D_A)[:, :seq].reshape(m, D_A)
    o_b, r_new = rwkv_mixer(pb, None if s_shift is None else s_shift[e],
                            None if s_rwkv is None else s_rwkv[e], p, e, seq)
    mix = matmul([o_a, o_b], p["w_out_even"], e, tn=1024)
    return mix, h_new, r_new, pb.reshape(bsz, seq, B_PROJ)[:, -1]


SB_TQ = 512
SB_TK = 128
SB_SCALE = HD_C ** -0.5


def _sb_block_sums(sp, lt2):
    hi, lo = _split2(sp)
    return (jnp.dot(hi, lt2, preferred_element_type=F32)
            + jnp.dot(lo, lt2, preferred_element_type=F32))


def _sb_prompt_body(bias_ref, q_ref, k_ref, v_ref, lt2_ref, o_ref, qs_ref, acc_ref, c_ref, *,
                    layer):
    tq, tk, hg = SB_TQ, SB_TK, HEAD_GROUP
    nd = tq // tk
    qi = pl.program_id(2)
    lt2 = lt2_ref[...]
    row = lax.broadcasted_iota(jnp.int32, (tq, tk), 0)
    lane = lax.broadcasted_iota(jnp.int32, (tq, tk), 1)
    qs_ref[...] = (q_ref[...] * SB_SCALE).astype(BF16)
    acc_ref[...] = jnp.zeros_like(acc_ref)
    c_ref[...] = jnp.zeros_like(c_ref)
    heads = range(hg)
    cols = [slice(hd * HD_C, (hd + 1) * HD_C) for hd in heads]
    biases = [bias_ref[layer, pl.program_id(1) * hg + hd] for hd in heads]

    def tile(k0, mask):
        keys = pl.ds(k0, tk)
        z = [lax.dot_general(qs_ref[:, cols[hd]], k_ref[keys, cols[hd]].astype(BF16), _NT,
                             preferred_element_type=F32) + biases[hd] for hd in heads]
        sp = [_softplus(z[hd]) for hd in heads]
        if mask is not None:
            sp = [jnp.where(mask, x, 0.0) for x in sp]
        sums = [_sb_block_sums(sp[hd], lt2) for hd in heads]
        for hd in heads:
            att = jnp.exp(z[hd] - sums[hd][:, 0:tk] - c_ref[hd])
            if mask is not None:
                att = jnp.where(mask, att, 0.0)
            acc_ref[:, cols[hd]] += jnp.dot(att.astype(BF16), v_ref[keys, cols[hd]].astype(BF16),
                                            preferred_element_type=F32)
        for hd in heads:
            c_ref[hd] += sums[hd][:, tk:2 * tk]

    for d in reversed(range(nd)):
        tile(pl.multiple_of(qi * tq + d * tk, tk), (d * tk + lane) < row)

    def full_tile(jj, carry):
        tile(pl.multiple_of((qi * nd - 1 - jj) * tk, tk), None)
        return carry

    lax.fori_loop(0, qi * nd, full_tile, 0)
    o_ref[...] = acc_ref[...].astype(o_ref.dtype)


def _lt2(tk):
    r = np.arange(tk)
    return jnp.asarray(np.concatenate([r[:, None] >= r[None, :], np.ones((tk, tk), bool)], axis=1),
                       BF16)


def sb_prompt(q, k, v, bias, layer, seq):
    m = q.shape[0]
    bsz = m // seq
    tq, hg = SB_TQ, HEAD_GROUP
    assert seq % tq == 0 and H_C % hg == 0
    nq = seq // tq
    ngrp = H_C // hg
    wide = hg * HD_C
    return pl.pallas_call(
        functools.partial(_sb_prompt_body, layer=layer),
        grid=(bsz, ngrp, nq),
        in_specs=[pl.BlockSpec(memory_space=pltpu.SMEM),
                  pl.BlockSpec((tq, wide), lambda b, h, i: (b * nq + i, h)),
                  pl.BlockSpec((seq, wide), lambda b, h, i: (b, h)),
                  pl.BlockSpec((seq, wide), lambda b, h, i: (b, h)),
                  pl.BlockSpec((SB_TK, 2 * SB_TK), lambda b, h, i: (0, 0))],
        out_specs=pl.BlockSpec((tq, wide), lambda b, h, i: (b * nq + i, h)),
        out_shape=jax.ShapeDtypeStruct((m, D_MODEL), BF16),
        scratch_shapes=[pltpu.VMEM((tq, wide), BF16), pltpu.VMEM((tq, wide), F32),
                        pltpu.VMEM((hg, tq, SB_TK), F32)],
        compiler_params=_cparams(3),
        name="sb_prompt",
    )(bias, q, k, v, _lt2(SB_TK))


SB_PAGES_PER_STEP = 8


def _sb_sample_body(pt_ref, q_ref, kn_ref, vn_ref, *rest):
    npg = SB_PAGES_PER_STEP
    kp_refs = rest[0:npg]
    vp_refs = rest[npg:2 * npg]
    bias_ref, lt2_ref, o_ref, qs_ref, acc_ref, c_ref = rest[2 * npg:]
    s = pl.program_id(1)
    nq = q_ref.shape[0]
    lt2 = lt2_ref[...]
    tk = PAGE_SIZE

    def block(k_ref, v_ref, new_keys):
        zs = []
        for h in range(H_C):
            cols = slice(h * HD_C, (h + 1) * HD_C)
            kh = k_ref[pl.ds(h, PAGE_SIZE, stride=H_C), :].astype(BF16)
            zs.append(lax.dot_general(qs_ref[:, cols].astype(BF16), kh, _NT,
                                      preferred_element_type=F32))
        z = jnp.concatenate(zs, axis=0) + bias_ref[...]
        sp = _softplus(z)
        if new_keys:
            s_idx = lax.broadcasted_iota(jnp.int32, z.shape, 1)
            q_idx = lax.broadcasted_iota(jnp.int32, z.shape, 0) % nq
            mask = s_idx < q_idx
            sp = jnp.where(mask, sp, 0.0)
        sums = _sb_block_sums(sp, lt2)
        att = jnp.exp(z - sums[:, 0:tk] - c_ref[...])
        if new_keys:
            att = jnp.where(mask, att, 0.0)
        for h in range(H_C):
            cols = slice(h * HD_C, (h + 1) * HD_C)
            vh = v_ref[pl.ds(h, PAGE_SIZE, stride=H_C), :].astype(BF16)
            acc_ref[:, cols] += jnp.dot(att[h * nq:(h + 1) * nq, :].astype(BF16), vh,
                                        preferred_element_type=F32)
        c_ref[...] += sums[:, tk:2 * tk]

    @pl.when(s == 0)
    def _():
        qs_ref[...] = q_ref[...] * SB_SCALE
        acc_ref[...] = jnp.zeros_like(acc_ref)
        c_ref[...] = jnp.zeros_like(c_ref)
        block(kn_ref, vn_ref, True)

    for j in range(npg):
        block(kp_refs[j], vp_refs[j], False)

    @pl.when(s == pl.num_programs(1) - 1)
    def _():
        o_ref[...] = acc_ref[...].astype(o_ref.dtype)


def sb_sample(q, k, v, cache_k, cache_v, page_table, bias, layer, seq):
    m = q.shape[0]
    bsz = m // seq
    n_pages = page_table.shape[1]
    npg = SB_PAGES_PER_STEP
    assert seq * H_C == LANES and seq <= PAGE_SIZE and n_pages % npg == 0
    rows = PAGE_SIZE * H_C
    ck = cache_k.reshape(cache_k.shape[0], cache_k.shape[1], rows, HD_C)
    cv = cache_v.reshape(cache_v.shape[0], cache_v.shape[1], rows, HD_C)
    pad = ((0, 0), (0, (PAGE_SIZE - seq) * H_C), (0, 0))
    kn = jnp.pad(k.reshape(bsz, seq * H_C, HD_C), pad)
    vn = jnp.pad(v.reshape(bsz, seq * H_C, HD_C), pad)
    bias_rows = jnp.broadcast_to(jnp.repeat(bias[layer], seq)[:, None], (LANES, PAGE_SIZE))
    new = pl.BlockSpec((None, rows, HD_C), lambda b, s, pt: (b, 0, 0))

    def page(j):
        return pl.BlockSpec((None, None, rows, HD_C),
                            lambda b, s, pt: (layer, pt[b, n_pages - 1 - (s * npg + j)], 0, 0))

    pages = [page(j) for j in range(npg)]
    grid_spec = pltpu.PrefetchScalarGridSpec(
        num_scalar_prefetch=1,
        grid=(bsz, n_pages // npg),
        in_specs=[pl.BlockSpec((seq, D_MODEL), lambda b, s, pt: (b, 0)), new, new] + pages + pages
        + [pl.BlockSpec((PAGE_SIZE, PAGE_SIZE), lambda b, s, pt: (0, 0)),
           pl.BlockSpec((PAGE_SIZE, 2 * PAGE_SIZE), lambda b, s, pt: (0, 0))],
        out_specs=pl.BlockSpec((seq, D_MODEL), lambda b, s, pt: (b, 0)),
        scratch_shapes=[pltpu.VMEM((seq, D_MODEL), F32), pltpu.VMEM((seq, D_MODEL), F32),
                        pltpu.VMEM((LANES, PAGE_SIZE), F32)],
    )
    return pl.pallas_call(
        _sb_sample_body,
        grid_spec=grid_spec,
        out_shape=jax.ShapeDtypeStruct((m, D_MODEL), BF16),
        compiler_params=_cparams(2),
        name="sb_sample",
    )(page_table, q, kn, vn, *([ck] * npg), *([cv] * npg), bias_rows, _lt2(PAGE_SIZE))


def sb_mixer(xb, seq, cache_k, cache_v, page_table, p, o):
    m = xb.shape[0]
    bsz = m // seq
    q, k, v = (matmul(xb, p["w_qkv_odd"], o, n_out=D_MODEL, col_off=c * D_MODEL, tn=1024)
               for c in range(3))
    if cache_k is None:
        att = sb_prompt(q, k, v, p["sb_bias"], o, seq)
    else:
        att = sb_sample(q, k, v, cache_k, cache_v, page_table, p["sb_bias"], o, seq)
    mix = matmul(att, p["w_out_odd"], o, tn=1024)
    return mix, k.reshape(bsz, seq, H_C, HD_C), v.reshape(bsz, seq, H_C, HD_C)


def conv_ffn(xb, seq, buf, p, layer):
    if buf is None:
        a, buf_new = ffn_up_fresh(xb, p["ffn_up"], p["ffn_conv_w"], p["ffn_conv_b"], layer, seq)
    else:
        h = matmul(xb, p["ffn_up"], layer, tn=1024)
        a, buf_new = conv_geglu(h, buf, p["ffn_conv_w"], p["ffn_conv_b"], layer, seq, tn=D_FF)
    return matmul(a, p["ffn_down"], layer, tm=512), buf_new


def _hgrn_lower_bounds(lb_param):
    lbs = jnp.cumsum(jax.nn.softmax(lb_param.astype(F32), axis=0), axis=0)
    return lbs - lbs[0:1]


def kernel(x_prompt, x_sample, cache_k, cache_v, page_table, state_hgrn, state_rwkv, state_rwkv_shift, state_ffn_conv, w_in_even, hgrn_lb_param, hgrn_norm_g, rwkv_mu, rwkv_w0, rwkv_w2, rwkv_a0, rwkv_a2, rwkv_g2, rwkv_k_k, rwkv_k_a, rwkv_r_k, rwkv_gn_g, rwkv_gn_b, w_out_even, w_qkv_odd, w_out_odd, sb_bias, ln1_g, ln1_b, ln2_g, ln2_b, ffn_up, ffn_conv_w, ffn_conv_b, ffn_down):
    p = dict(w_in_even=w_in_even, hgrn_norm_g=hgrn_norm_g, rwkv_mu=rwkv_mu, rwkv_w0=rwkv_w0,
             rwkv_w2=rwkv_w2, rwkv_a0=rwkv_a0, rwkv_a2=rwkv_a2, rwkv_g2=rwkv_g2, rwkv_k_k=rwkv_k_k,
             rwkv_k_a=rwkv_k_a, rwkv_r_k=rwkv_r_k, rwkv_gn_g=rwkv_gn_g, rwkv_gn_b=rwkv_gn_b,
             w_out_even=w_out_even, w_qkv_odd=w_qkv_odd, w_out_odd=w_out_odd, sb_bias=sb_bias,
             ffn_up=ffn_up, ffn_conv_w=ffn_conv_w, ffn_conv_b=ffn_conv_b, ffn_down=ffn_down)
    lbs = _hgrn_lower_bounds(hgrn_lb_param)
    bp, lp, _ = x_prompt.shape
    bs, ls, _ = x_sample.shape
    groups = [
        dict(x=x_prompt.reshape(bp * lp, D_MODEL), seq=lp, sample=False),
        dict(x=x_sample.reshape(bs * ls, D_MODEL), seq=ls, sample=True),
    ]
    for g in groups:
        g["xb"] = g["x"].astype(BF16)
        g["k"], g["v"], g["hgrn"], g["rwkv"], g["shift"], g["conv"] = [], [], [], [], [], []
    for layer in range(DEPTH):
        for g in groups:
            seq, sample = g["seq"], g["sample"]
            if layer % 2 == 0:
                e = layer // 2
                mix, h_new, r_new, sh_new = even_mixer(
                    g["xb"], seq, state_hgrn if sample else None, state_rwkv if sample else None,
                    state_rwkv_shift if sample else None, lbs, p, e)
                g["hgrn"].append(h_new)
                g["rwkv"].append(r_new)
                g["shift"].append(sh_new)
            else:
                o = layer // 2
                mix, k_new, v_new = sb_mixer(g["xb"], seq, cache_k if sample else None,
                                             cache_v if sample else None, page_table, p, o)
                g["k"].append(k_new)
                g["v"].append(v_new)
            g["x"], g["xb"] = add_layer_norm(g["x"], mix, ln1_g, ln1_b, layer)
            f, c_new = conv_ffn(g["xb"], seq, state_ffn_conv if sample else None, p, layer)
            g["conv"].append(c_new)
            g["x"], g["xb"] = add_layer_norm(g["x"], f, ln2_g, ln2_b, layer)
    gp, gs = groups
    st = jnp.stack
    return (gp["x"].reshape(bp, lp, D_MODEL), gs["x"].reshape(bs, ls, D_MODEL),
            st(gp["k"]), st(gp["v"]), st(gs["k"]), st(gs["v"]),
            st(gp["hgrn"]), st(gs["hgrn"]), st(gp["rwkv"]), st(gs["rwkv"]),
            st(gp["shift"]), st(gs["shift"]), st(gp["conv"]), st(gs["conv"]))
```
